```python
import math
import jax, jax.numpy as jnp
from jax import lax
import numpy as np

D_MODEL = 1024
BATCH = 32
SEQ = 2048
DEPTH = 1
DEC_BATCH = 32
DEC_SEQ = 32
PAST_LEN = 2048

CHUNK = 64
SSM_WIDTH = D_MODEL // 2
SSM_GROUP_CH = 16
SSM_GROUPS = SSM_WIDTH // SSM_GROUP_CH
SSM_STATE = 64
POOL_WIDTH = D_MODEL // 2
POOL_WINDOWS = (2, 4, 8, 16)
POOL_GROUPS = len(POOL_WINDOWS)
POOL_GROUP_CH = POOL_WIDTH // POOL_GROUPS
POOL_OUT_CH = D_MODEL // POOL_GROUPS
POOL_HIST = max(POOL_WINDOWS) - 1
IN_WIDTH = SSM_WIDTH + POOL_WIDTH + 2 * D_MODEL
D_FF = -(-8 * D_MODEL // (3 * 256)) * 256
RMS_EPS = 1e-6
A_RE_MAX = -1e-4

kernel_name = "s5_pool_gated_hybrid_stream_step"


def rmsnorm(x, g):
    xf = x.astype(jnp.float32)
    y = xf * lax.rsqrt(jnp.mean(xf * xf, axis=-1, keepdims=True) + RMS_EPS) * g.astype(jnp.float32)
    return y.astype(x.dtype)


def ssm_branch(u, h0_re, h0_im, a_re, a_im, log_dt, b_re, b_im, c_re, c_im, d_skip, w_glu_a, w_glu_b):
    bsz, t_len, _ = u.shape
    f32 = jnp.float32
    uf = u.astype(f32).reshape(bsz, t_len, SSM_GROUPS, SSM_GROUP_CH)
    lam = lax.complex(jnp.minimum(a_re.astype(f32), A_RE_MAX), a_im.astype(f32))
    dt = jnp.exp(log_dt.astype(f32))[:, None]
    a_bar = jnp.exp(lam * dt)
    b_mat = lax.complex(b_re.astype(f32), b_im.astype(f32))
    b_bar = ((a_bar - 1.0) / lam)[..., None] * b_mat
    c_mat = lax.complex(c_re.astype(f32), c_im.astype(f32))
    bu = jnp.einsum('btgh,gph->btgp', uf, b_bar)
    h0 = lax.complex(h0_re.astype(f32), h0_im.astype(f32))
    bu = bu.at[:, 0].add(a_bar[None] * h0)
    a_seq = jnp.broadcast_to(a_bar, (1, t_len, SSM_GROUPS, SSM_STATE))

    def combine(left, right):
        a_l, b_l = left
        a_r, b_r = right
        return a_l * a_r, a_r * b_l + b_r

    _, hs = lax.associative_scan(combine, (a_seq, bu), axis=1)
    y = jnp.real(jnp.einsum('btgp,ghp->btgh', hs, c_mat)) + d_skip.astype(f32) * uf
    y = jax.nn.gelu(y).reshape(bsz, t_len, SSM_WIDTH).astype(u.dtype)
    out = (y @ w_glu_a) * jax.nn.sigmoid(y @ w_glu_b)
    h_last = hs[:, -1]
    return out, jnp.real(h_last), jnp.imag(h_last)


def pool_branch(u, hist, pos0, pool_w, pool_scale):
    bsz, t_len, _ = u.shape
    full = jnp.concatenate([hist.astype(u.dtype), u], axis=1)
    fullf = full.astype(jnp.float32)
    cs = jnp.concatenate([jnp.zeros((bsz, 1, POOL_WIDTH), jnp.float32),
                          jnp.cumsum(fullf, axis=1)], axis=1)
    uf = u.astype(jnp.float32)
    pos = pos0 + jnp.arange(t_len)
    groups = []
    for gi, w in enumerate(POOL_WINDOWS):
        lo, hi = gi * POOL_GROUP_CH, (gi + 1) * POOL_GROUP_CH
        end = cs[:, POOL_HIST + 1:POOL_HIST + 1 + t_len, lo:hi]
        start = cs[:, POOL_HIST + 1 - w:POOL_HIST + 1 - w + t_len, lo:hi]
        cnt = jnp.minimum(pos + 1, w).astype(jnp.float32)[None, :, None]
        groups.append((end - start) / cnt - uf[..., lo:hi])
    pooled = jnp.stack(groups, axis=2).astype(u.dtype)
    out = jnp.einsum('btgc,gcd->btgd', pooled, pool_w).reshape(bsz, t_len, D_MODEL) * pool_scale
    return out, full[:, -POOL_HIST:]


def trunk(x, h_re, h_im, pool_hist, pos0, g_mix, w_in, ssm_a_re, ssm_a_im, ssm_log_dt,
          ssm_b_re, ssm_b_im, ssm_c_re, ssm_c_im, ssm_d, w_glu_a, w_glu_b, pool_w, pool_scale,
          w_out, g_ffn, w_ffn_gate, w_ffn_up, w_ffn_down, g_final):
    new_re, new_im, new_pool = [], [], []
    h = x
    for l in range(DEPTH):
        n = rmsnorm(h, g_mix[l])
        z = n @ w_in[l]
        u_ssm = z[..., :SSM_WIDTH]
        u_pool = z[..., SSM_WIDTH:SSM_WIDTH + POOL_WIDTH]
        gate_ssm = z[..., SSM_WIDTH + POOL_WIDTH:SSM_WIDTH + POOL_WIDTH + D_MODEL]
        gate_pool = z[..., SSM_WIDTH + POOL_WIDTH + D_MODEL:]
        a_out, s_re, s_im = ssm_branch(u_ssm, h_re[l], h_im[l], ssm_a_re[l], ssm_a_im[l],
                                       ssm_log_dt[l], ssm_b_re[l], ssm_b_im[l], ssm_c_re[l],
                                       ssm_c_im[l], ssm_d[l], w_glu_a[l], w_glu_b[l])
        p_out, p_hist = pool_branch(u_pool, pool_hist[l], pos0, pool_w[l], pool_scale[l])
        merged = jax.nn.sigmoid(gate_ssm) * a_out + jax.nn.sigmoid(gate_pool) * p_out
        h = h + merged @ w_out[l]
        n2 = rmsnorm(h, g_ffn[l])
        h = h + (jax.nn.silu(n2 @ w_ffn_gate[l]) * (n2 @ w_ffn_up[l])) @ w_ffn_down[l]
        new_re.append(s_re.astype(h_re.dtype))
        new_im.append(s_im.astype(h_im.dtype))
        new_pool.append(p_hist.astype(pool_hist.dtype))
    y = rmsnorm(h, g_final)
    return y, jnp.stack(new_re), jnp.stack(new_im), jnp.stack(new_pool)


def setup_inputs(seed: int = 0) -> dict:
    key = jax.random.key(seed)
    ks = jax.random.split(key, 32)
    f32 = jnp.float32

    def nrm(k, shape, scale):
        return jax.random.normal(k, shape, f32) * scale

    n_idx = jnp.arange(SSM_STATE, dtype=f32)
    return {
        "x_prompt": nrm(ks[0], (BATCH, SEQ, D_MODEL), 1.0),
        "x_sample": nrm(ks[1], (DEC_BATCH, DEC_SEQ, D_MODEL), 1.0),
        "state_ssm_re": nrm(ks[2], (DEPTH, DEC_BATCH, SSM_GROUPS, SSM_STATE), 0.1),
        "state_ssm_im": nrm(ks[3], (DEPTH, DEC_BATCH, SSM_GROUPS, SSM_STATE), 0.1),
        "state_pool": nrm(ks[4], (DEPTH, DEC_BATCH, POOL_HIST, POOL_WIDTH), 1.0),
        "g_mix": 1.0 + nrm(ks[5], (DEPTH, D_MODEL), 0.02),
        "w_in": nrm(ks[6], (DEPTH, D_MODEL, IN_WIDTH), D_MODEL ** -0.5),
        "ssm_a_re": -0.5 + nrm(ks[7], (DEPTH, SSM_GROUPS, SSM_STATE), 0.01),
        "ssm_a_im": math.pi * n_idx + nrm(ks[8], (DEPTH, SSM_GROUPS, SSM_STATE), 0.01),
        "ssm_log_dt": jax.random.uniform(ks[9], (DEPTH, SSM_GROUPS), f32,
                                         math.log(1e-3), math.log(1e-1)),
        "ssm_b_re": nrm(ks[10], (DEPTH, SSM_GROUPS, SSM_STATE, SSM_GROUP_CH), (2 * SSM_GROUP_CH) ** -0.5),
        "ssm_b_im": nrm(ks[11], (DEPTH, SSM_GROUPS, SSM_STATE, SSM_GROUP_CH), (2 * SSM_GROUP_CH) ** -0.5),
        "ssm_c_re": nrm(ks[12], (DEPTH, SSM_GROUPS, SSM_GROUP_CH, SSM_STATE), (2 * SSM_STATE) ** -0.5),
        "ssm_c_im": nrm(ks[13], (DEPTH, SSM_GROUPS, SSM_GROUP_CH, SSM_STATE), (2 * SSM_STATE) ** -0.5),
        "ssm_d": nrm(ks[14], (DEPTH, SSM_GROUPS, SSM_GROUP_CH), 1.0),
        "w_glu_a": nrm(ks[15], (DEPTH, SSM_WIDTH, D_MODEL), SSM_WIDTH ** -0.5),
        "w_glu_b": nrm(ks[16], (DEPTH, SSM_WIDTH, D_MODEL), SSM_WIDTH ** -0.5),
        "pool_w": nrm(ks[17], (DEPTH, POOL_GROUPS, POOL_GROUP_CH, POOL_OUT_CH), POOL_GROUP_CH ** -0.5),
        "pool_scale": 1.0 + nrm(ks[18], (DEPTH, D_MODEL), 0.02),
        "w_out": nrm(ks[19], (DEPTH, D_MODEL, D_MODEL), D_MODEL ** -0.5),
        "g_ffn": 1.0 + nrm(ks[20], (DEPTH, D_MODEL), 0.02),
        "w_ffn_gate": nrm(ks[21], (DEPTH, D_MODEL, D_FF), D_MODEL ** -0.5),
        "w_ffn_up": nrm(ks[22], (DEPTH, D_MODEL, D_FF), D_MODEL ** -0.5),
        "w_ffn_down": nrm(ks[23], (DEPTH, D_FF, D_MODEL), D_FF ** -0.5),
        "g_final": 1.0 + nrm(ks[24], (D_MODEL,), 0.02),
    }


def reference(x_prompt, x_sample, state_ssm_re, state_ssm_im, state_pool, g_mix, w_in,
              ssm_a_re, ssm_a_im, ssm_log_dt, ssm_b_re, ssm_b_im, ssm_c_re, ssm_c_im, ssm_d,
              w_glu_a, w_glu_b, pool_w, pool_scale, w_out, g_ffn, w_ffn_gate, w_ffn_up,
              w_ffn_down, g_final):
    weights = (g_mix, w_in, ssm_a_re, ssm_a_im, ssm_log_dt, ssm_b_re, ssm_b_im, ssm_c_re,
               ssm_c_im, ssm_d, w_glu_a, w_glu_b, pool_w, pool_scale, w_out, g_ffn,
               w_ffn_gate, w_ffn_up, w_ffn_down, g_final)
    bp = x_prompt.shape[0]
    zero_ssm = jnp.zeros((DEPTH, bp, SSM_GROUPS, SSM_STATE), state_ssm_re.dtype)
    zero_pool = jnp.zeros((DEPTH, bp, POOL_HIST, POOL_WIDTH), state_pool.dtype)
    y_prompt, p_re, p_im, p_pool = trunk(x_prompt, zero_ssm, zero_ssm, zero_pool, 0, *weights)
    y_sample, s_re, s_im, s_pool = trunk(x_sample, state_ssm_re, state_ssm_im, state_pool,
                                         PAST_LEN, *weights)
    return (y_prompt, y_sample, p_re, p_im, p_pool, s_re, s_im, s_pool)
```

```python
import functools

import jax
import jax.numpy as jnp
from jax import lax
from jax.experimental import pallas as pl
from jax.experimental.pallas import tpu as pltpu

D_MODEL = 1024
SSM_WIDTH = 512
SSM_GROUP_CH = 16
SSM_GROUPS = 32
SSM_STATE = 64
N_STATE = SSM_GROUPS * SSM_STATE
POOL_WIDTH = 512
POOL_WINDOWS = (2, 4, 8, 16)
POOL_GROUP_CH = 128
POOL_OUT_CH = 256
POOL_HIST = 15
HIST_ROWS = 16
IN_WIDTH = SSM_WIDTH + POOL_WIDTH + 2 * D_MODEL
D_FF = 2816
RMS_EPS = 1e-6
A_RE_MAX = -1e-4

SUBLANES = 8
LANES = 128
MXU_DIM = 256
STREAMS = SUBLANES
SSM_HALVES = SSM_WIDTH // MXU_DIM
HALF_STATE = N_STATE // SSM_HALVES
SCAN_COLS = 512
MIXER_VMEM_BYTES = 52 * 1024 * 1024
FFN_VMEM_BYTES = 52 * 1024 * 1024

_F32 = jnp.float32
_BF16 = jnp.bfloat16


def _dot(a, b):
    return jnp.dot(a, b, preferred_element_type=_F32)


def _rmsnorm(x, g):
    ms = jnp.mean(x * x, axis=-1, keepdims=True)
    return x * lax.rsqrt(ms + RMS_EPS) * g


def _discretize_kernel(are_ref, aim_ref, ldt_ref, bre_ref, bim_ref,
                       oar_ref, oai_ref, obr_ref, obi_ref):
    lr = jnp.minimum(are_ref[...], A_RE_MAX)
    li = aim_ref[...]
    dt = jnp.exp(ldt_ref[...])
    mag = jnp.exp(lr * dt)
    ar = mag * jnp.cos(li * dt)
    ai = mag * jnp.sin(li * dt)
    xr = ar - 1.0
    den = lr * lr + li * li
    cr = (xr * lr + ai * li) / den
    ci = (ai * lr - xr * li) / den
    br = bre_ref[...]
    bi = bim_ref[...]
    oar_ref[...] = ar
    oai_ref[...] = ai
    obr_ref[...] = cr * br - ci * bi
    obi_ref[...] = cr * bi + ci * br


def _discretize(a_re, a_im, log_dt, b_re, b_im):
    shape = b_re.shape
    flat = (shape[0] * shape[1] * shape[2] // LANES, LANES)

    def spread(v):
        return jnp.broadcast_to(v, shape).reshape(flat)

    args = (spread(a_re[:, :, None]), spread(a_im[:, :, None]), spread(log_dt[:, None, None]),
            b_re.reshape(flat), b_im.reshape(flat))
    outs = pl.pallas_call(
        _discretize_kernel,
        out_shape=[jax.ShapeDtypeStruct(flat, _F32)] * 4,
        name="discretize",
    )(*args)
    ar, ai, br, bi = (o.reshape(shape) for o in outs)
    return ar[:, :, 0], ai[:, :, 0], br, bi


def _mixer_kernel(x_ref, h0re_ref, h0im_ref, pool0_ref, gmix_ref, win_ref, are_ref, aim_ref,
                  bblk_ref, cre_ref, cim_ref, dskip_ref, wga_ref, wgb_ref, poolw_ref,
                  pscale_ref, wout_ref,
                  h_ref, sre_ref, sim_ref, poolout_ref,
                  hs_ref, st_ref, pbuf_ref, gate_ref, u_ref, *, tc, pos0):
    j = pl.program_id(1)
    rows = tc * STREAMS

    @pl.when(j == 0)
    def _load_state():
        for k in range(SSM_HALVES):
            src = slice(k * HALF_STATE, (k + 1) * HALF_STATE)
            st_ref[:, 2 * k * HALF_STATE:(2 * k + 1) * HALF_STATE] = h0re_ref[:, src]
            st_ref[:, (2 * k + 1) * HALF_STATE:(2 * k + 2) * HALF_STATE] = h0im_ref[:, src]
        pbuf_ref[0:HIST_ROWS] = pool0_ref[...]

    x = x_ref[...].reshape(rows, D_MODEL)
    n = _rmsnorm(x, gmix_ref[...]).astype(_BF16)

    u_ref[...] = _dot(n, win_ref[:, 0:SSM_WIDTH])
    pbuf_ref[HIST_ROWS:HIST_ROWS + tc] = _dot(
        n, win_ref[:, SSM_WIDTH:SSM_WIDTH + POOL_WIDTH]).reshape(tc, STREAMS, POOL_WIDTH)
    gate0 = SSM_WIDTH + POOL_WIDTH
    for c in range(2 * D_MODEL // 512):
        gate_ref[:, c * 512:(c + 1) * 512] = jax.nn.sigmoid(
            _dot(n, win_ref[:, gate0 + c * 512:gate0 + (c + 1) * 512]))

    ub = u_ref[...].astype(_BF16)
    for k in range(SSM_HALVES):
        hs_ref[:, 2 * k * HALF_STATE:(2 * k + 2) * HALF_STATE] = _dot(
            ub[:, k * MXU_DIM:(k + 1) * MXU_DIM], bblk_ref[k])

    for k in range(SSM_HALVES):
        for c0 in range(0, HALF_STATE, SCAN_COLS):
            re_cols = slice(2 * k * HALF_STATE + c0, 2 * k * HALF_STATE + c0 + SCAN_COLS)
            im_cols = slice((2 * k + 1) * HALF_STATE + c0, (2 * k + 1) * HALF_STATE + c0 + SCAN_COLS)
            a_cols = slice(k * HALF_STATE + c0, k * HALF_STATE + c0 + SCAN_COLS)
            ar = jnp.broadcast_to(are_ref[:, a_cols], (STREAMS, SCAN_COLS))
            ai = jnp.broadcast_to(aim_ref[:, a_cols], (STREAMS, SCAN_COLS))

            def step(t, carry, ar=ar, ai=ai, re_cols=re_cols, im_cols=im_cols):
                hr, hi = carry
                r0 = pl.multiple_of(t * STREAMS, STREAMS)
                nr = ar * hr - ai * hi + hs_ref[pl.ds(r0, STREAMS), re_cols]
                ni = ar * hi + ai * hr + hs_ref[pl.ds(r0, STREAMS), im_cols]
                hs_ref[pl.ds(r0, STREAMS), re_cols] = nr
                hs_ref[pl.ds(r0, STREAMS), im_cols] = ni
                return nr, ni

            hr, hi = lax.fori_loop(0, tc, step, (st_ref[:, re_cols], st_ref[:, im_cols]),
                                   unroll=4)
            st_ref[:, re_cols] = hr
            st_ref[:, im_cols] = hi

    ys = []
    for k in range(SSM_HALVES):
        hre = hs_ref[:, 2 * k * HALF_STATE:(2 * k + 1) * HALF_STATE].astype(_BF16)
        him = hs_ref[:, (2 * k + 1) * HALF_STATE:(2 * k + 2) * HALF_STATE].astype(_BF16)
        ys.append(_dot(hre, cre_ref[k]) - _dot(him, cim_ref[k]))
    y = jnp.concatenate(ys, axis=1) + dskip_ref[...] * u_ref[...]
    yb = jax.nn.gelu(y).astype(_BF16)
    a_out = _dot(yb, wga_ref[...]) * jax.nn.sigmoid(_dot(yb, wgb_ref[...]))

    pos = pos0 + j * tc + lax.broadcasted_iota(jnp.int32, (tc, STREAMS, POOL_GROUP_CH), 0)
    merged = []
    for gi, w in enumerate(POOL_WINDOWS):
        c = pbuf_ref[:, :, gi * POOL_GROUP_CH:(gi + 1) * POOL_GROUP_CH]
        s, width = c, 1
        while width < w:
            s = s[:s.shape[0] - width] + s[width:]
            width *= 2
        first = HIST_ROWS + 1 - w
        wsum = s[first:first + tc]
        cnt = jnp.minimum(pos + 1, w).astype(_F32)
        pooled = wsum / cnt - c[HIST_ROWS:HIST_ROWS + tc]
        pb = pooled.reshape(rows, POOL_GROUP_CH).astype(_BF16)
        cols = slice(gi * POOL_OUT_CH, (gi + 1) * POOL_OUT_CH)
        p_out = _dot(pb, poolw_ref[gi]) * pscale_ref[:, cols]
        merged.append(gate_ref[:, cols] * a_out[:, cols]
                      + gate_ref[:, D_MODEL + gi * POOL_OUT_CH:D_MODEL + (gi + 1) * POOL_OUT_CH] * p_out)
    mb = jnp.concatenate(merged, axis=1).astype(_BF16)
    h = x + _dot(mb, wout_ref[...])
    h_ref[...] = h.reshape(tc, STREAMS, D_MODEL)

    tail = pbuf_ref[tc:tc + HIST_ROWS]
    pbuf_ref[0:HIST_ROWS] = tail

    @pl.when(j == pl.num_programs(1) - 1)
    def _store_state():
        for k in range(SSM_HALVES):
            dst = slice(k * HALF_STATE, (k + 1) * HALF_STATE)
            sre_ref[:, dst] = st_ref[:, 2 * k * HALF_STATE:(2 * k + 1) * HALF_STATE]
            sim_ref[:, dst] = st_ref[:, (2 * k + 1) * HALF_STATE:(2 * k + 2) * HALF_STATE]
        poolout_ref[...] = tail


def _mixer(x_t, h0_re, h0_im, pool0, w, *, tc, pos0):
    t_len, bsz, _ = x_t.shape
    rows = tc * STREAMS
    grid = (bsz // STREAMS, t_len // tc)

    def full(a):
        nd = a.ndim
        return pl.BlockSpec(a.shape, lambda i, j, nd=nd: (0,) * nd)

    weights = (w["g_mix"], w["w_in"], w["abar_re"], w["abar_im"], w["bblk"], w["cre"], w["cim"],
               w["d_skip"], w["w_glu_a"], w["w_glu_b"], w["pool_w"], w["pool_scale"], w["w_out"])
    in_specs = [
        pl.BlockSpec((tc, STREAMS, D_MODEL), lambda i, j: (j, i, 0)),
        pl.BlockSpec((STREAMS, N_STATE), lambda i, j: (i, 0)),
        pl.BlockSpec((STREAMS, N_STATE), lambda i, j: (i, 0)),
        pl.BlockSpec((HIST_ROWS, STREAMS, POOL_WIDTH), lambda i, j: (0, i, 0)),
    ] + [full(a) for a in weights]
    out_specs = [
        pl.BlockSpec((tc, STREAMS, D_MODEL), lambda i, j: (j, i, 0)),
        pl.BlockSpec((STREAMS, N_STATE), lambda i, j: (i, 0)),
        pl.BlockSpec((STREAMS, N_STATE), lambda i, j: (i, 0)),
        pl.BlockSpec((HIST_ROWS, STREAMS, POOL_WIDTH), lambda i, j: (0, i, 0)),
    ]
    out_shape = [
        jax.ShapeDtypeStruct((t_len, bsz, D_MODEL), _F32),
        jax.ShapeDtypeStruct((bsz, N_STATE), _F32),
        jax.ShapeDtypeStruct((bsz, N_STATE), _F32),
        jax.ShapeDtypeStruct((HIST_ROWS, bsz, POOL_WIDTH), _F32),
    ]
    scratch = [
        pltpu.VMEM((rows, 2 * N_STATE), _F32),
        pltpu.VMEM((STREAMS, 2 * N_STATE), _F32),
        pltpu.VMEM((HIST_ROWS + tc, STREAMS, POOL_WIDTH), _F32),
        pltpu.VMEM((rows, 2 * D_MODEL), _F32),
        pltpu.VMEM((rows, SSM_WIDTH), _F32),
    ]
    return pl.pallas_call(
        functools.partial(_mixer_kernel, tc=tc, pos0=pos0),
        grid=grid,
        in_specs=in_specs,
        out_specs=out_specs,
        out_shape=out_shape,
        scratch_shapes=scratch,
        compiler_params=pltpu.CompilerParams(
            dimension_semantics=("arbitrary", "arbitrary"),
            vmem_limit_bytes=MIXER_VMEM_BYTES),
        name="mixer",
    )(x_t, h0_re, h0_im, pool0, *weights)


def _ffn_kernel(h_ref, gffn_ref, wg_ref, wu_ref, wd_ref, gfin_ref, y_ref, act_ref):
    h = h_ref[...]
    n2 = _rmsnorm(h, gffn_ref[...]).astype(_BF16)
    for c in range(D_FF // MXU_DIM):
        cols = slice(c * MXU_DIM, (c + 1) * MXU_DIM)
        act_ref[:, cols] = (jax.nn.silu(_dot(n2, wg_ref[:, cols]))
                            * _dot(n2, wu_ref[:, cols])).astype(_BF16)
    out = h + _dot(act_ref[...], wd_ref[...])
    y_ref[...] = _rmsnorm(out, gfin_ref[...])


def _ffn(h2d, w, *, tm):
    n_rows = h2d.shape[0]

    def full(a):
        nd = a.ndim
        return pl.BlockSpec(a.shape, lambda i, nd=nd: (0,) * nd)

    weights = (w["g_ffn"], w["w_ffn_gate"], w["w_ffn_up"], w["w_ffn_down"], w["g_final"])
    return pl.pallas_call(
        _ffn_kernel,
        grid=(n_rows // tm,),
        in_specs=[pl.BlockSpec((tm, D_MODEL), lambda i: (i, 0))] + [full(a) for a in weights],
        out_specs=pl.BlockSpec((tm, D_MODEL), lambda i: (i, 0)),
        out_shape=jax.ShapeDtypeStruct((n_rows, D_MODEL), _F32),
        scratch_shapes=[pltpu.VMEM((tm, D_FF), _BF16)],
        compiler_params=pltpu.CompilerParams(
            dimension_semantics=("arbitrary",),
            vmem_limit_bytes=FFN_VMEM_BYTES),
        name="ffn",
    )(h2d, *weights)


def _block_diag(m):
    g, r, c = m.shape
    eye = jnp.eye(g, dtype=m.dtype)
    return (m[:, :, None, :] * eye[:, None, :, None]).reshape(g * r, g * c)


def _prepare_weights(g_mix, w_in, ssm_a_re, ssm_a_im, ssm_log_dt, ssm_b_re, ssm_b_im, ssm_c_re,
                     ssm_c_im, ssm_d, w_glu_a, w_glu_b, pool_w, pool_scale, w_out, g_ffn,
                     w_ffn_gate, w_ffn_up, w_ffn_down, g_final):
    abar_re, abar_im, bbar_re, bbar_im = _discretize(ssm_a_re, ssm_a_im, ssm_log_dt,
                                                    ssm_b_re, ssm_b_im)
    gpb = SSM_GROUPS // SSM_HALVES
    bblk, cre, cim = [], [], []
    for k in range(SSM_HALVES):
        gs = slice(k * gpb, (k + 1) * gpb)
        b_r = _block_diag(jnp.swapaxes(bbar_re[gs], 1, 2))
        b_i = _block_diag(jnp.swapaxes(bbar_im[gs], 1, 2))
        bblk.append(jnp.concatenate([b_r, b_i], axis=1))
        cre.append(_block_diag(jnp.swapaxes(ssm_c_re[gs], 1, 2)))
        cim.append(_block_diag(jnp.swapaxes(ssm_c_im[gs], 1, 2)))
    return {
        "g_mix": g_mix.reshape(1, D_MODEL),
        "w_in": w_in.astype(_BF16),
        "abar_re": abar_re.reshape(1, N_STATE),
        "abar_im": abar_im.reshape(1, N_STATE),
        "bblk": jnp.stack(bblk).astype(_BF16),
        "cre": jnp.stack(cre).astype(_BF16),
        "cim": jnp.stack(cim).astype(_BF16),
        "d_skip": ssm_d.reshape(1, SSM_WIDTH),
        "w_glu_a": w_glu_a.astype(_BF16),
        "w_glu_b": w_glu_b.astype(_BF16),
        "pool_w": pool_w.astype(_BF16),
        "pool_scale": pool_scale.reshape(1, D_MODEL),
        "w_out": w_out.astype(_BF16),
        "g_ffn": g_ffn.reshape(1, D_MODEL),
        "w_ffn_gate": w_ffn_gate.astype(_BF16),
        "w_ffn_up": w_ffn_up.astype(_BF16),
        "w_ffn_down": w_ffn_down.astype(_BF16),
        "g_final": g_final.reshape(1, D_MODEL),
    }


def _trunk(x, h_re, h_im, pool_hist, w, *, pos0, tc, tm):
    bsz, t_len, _ = x.shape
    x_t = jnp.swapaxes(x, 0, 1)
    pool0 = jnp.pad(jnp.swapaxes(pool_hist, 0, 1), ((HIST_ROWS - POOL_HIST, 0), (0, 0), (0, 0)))
    h_t, s_re, s_im, pool_t = _mixer(x_t, h_re.reshape(bsz, N_STATE), h_im.reshape(bsz, N_STATE),
                                     pool0, w, tc=tc, pos0=pos0)
    y_t = _ffn(h_t.reshape(t_len * bsz, D_MODEL), w, tm=tm).reshape(t_len, bsz, D_MODEL)
    new_pool = jnp.swapaxes(pool_t[HIST_ROWS - POOL_HIST:], 0, 1)
    return (jnp.swapaxes(y_t, 0, 1),
            s_re.reshape(1, bsz, SSM_GROUPS, SSM_STATE),
            s_im.reshape(1, bsz, SSM_GROUPS, SSM_STATE),
            new_pool[None])


def kernel(x_prompt, x_sample, state_ssm_re, state_ssm_im, state_pool, g_mix, w_in, ssm_a_re, ssm_a_im, ssm_log_dt, ssm_b_re, ssm_b_im, ssm_c_re, ssm_c_im, ssm_d, w_glu_a, w_glu_b, pool_w, pool_scale, w_out, g_ffn, w_ffn_gate, w_ffn_up, w_ffn_down, g_final):
    w = _prepare_weights(g_mix[0], w_in[0], ssm_a_re[0], ssm_a_im[0], ssm_log_dt[0], ssm_b_re[0],
                         ssm_b_im[0], ssm_c_re[0], ssm_c_im[0], ssm_d[0], w_glu_a[0], w_glu_b[0],
                         pool_w[0], pool_scale[0], w_out[0], g_ffn[0], w_ffn_gate[0], w_ffn_up[0],
                         w_ffn_down[0], g_final)
    bp = x_prompt.shape[0]
    zero_ssm = jnp.zeros((bp, SSM_GROUPS, SSM_STATE), state_ssm_re.dtype)
    zero_pool = jnp.zeros((bp, POOL_HIST, POOL_WIDTH), state_pool.dtype)
    y_p, p_re, p_im, p_pool = _trunk(x_prompt, zero_ssm, zero_ssm, zero_pool, w,
                                     pos0=0, tc=64, tm=512)
    past_len = 2048
    y_s, s_re, s_im, s_pool = _trunk(x_sample, state_ssm_re[0], state_ssm_im[0], state_pool[0], w,
                                     pos0=past_len, tc=x_sample.shape[1], tm=512)
    return (y_p, y_s, p_re, p_im, p_pool, s_re, s_im, s_pool)
```

```python
import functools

import jax
import jax.numpy as jnp
from jax import lax
from jax.experimental import pallas as pl
from jax.experimental.pallas import tpu as pltpu

D_MODEL = 1024
SSM_WIDTH = 512
SSM_GROUP_CH = 16
SSM_GROUPS = 32
SSM_STATE = 64
N_STATE = SSM_GROUPS * SSM_STATE
POOL_WIDTH = 512
POOL_WINDOWS = (2, 4, 8, 16)
POOL_GROUP_CH = 128
POOL_OUT_CH = 256
POOL_HIST = 15
HIST_ROWS = 16
IN_WIDTH = SSM_WIDTH + POOL_WIDTH + 2 * D_MODEL
D_FF = 2816
RMS_EPS = 1e-6
A_RE_MAX = -1e-4

SUBLANES = 8
LANES = 128
MXU_DIM = 256
STREAMS = SUBLANES
SSM_HALVES = SSM_WIDTH // MXU_DIM
HALF_STATE = N_STATE // SSM_HALVES
SCAN_COLS = 512
MIXER_VMEM_BYTES = 52 * 1024 * 1024
FFN_VMEM_BYTES = 52 * 1024 * 1024

_F32 = jnp.float32
_BF16 = jnp.bfloat16


def _dot(a, b):
    return jnp.dot(a, b, preferred_element_type=_F32)


def _rmsnorm(x, g):
    ms = jnp.mean(x * x, axis=-1, keepdims=True)
    return x * lax.rsqrt(ms + RMS_EPS) * g


def _discretize_kernel(are_ref, aim_ref, ldt_ref, bre_ref, bim_ref,
                       oar_ref, oai_ref, obr_ref, obi_ref):
    lr = jnp.minimum(are_ref[...], A_RE_MAX)
    li = aim_ref[...]
    dt = jnp.exp(ldt_ref[...])
    mag = jnp.exp(lr * dt)
    ar = mag * jnp.cos(li * dt)
    ai = mag * jnp.sin(li * dt)
    xr = ar - 1.0
    den = lr * lr + li * li
    cr = (xr * lr + ai * li) / den
    ci = (ai * lr - xr * li) / den
    br = bre_ref[...]
    bi = bim_ref[...]
    oar_ref[...] = ar
    oai_ref[...] = ai
    obr_ref[...] = cr * br - ci * bi
    obi_ref[...] = cr * bi + ci * br


def _discretize(a_re, a_im, log_dt, b_re, b_im):
    shape = b_re.shape
    flat = (shape[0] * shape[1] * shape[2] // LANES, LANES)

    def spread(v):
        return jnp.broadcast_to(v, shape).reshape(flat)

    args = (spread(a_re[:, :, None]), spread(a_im[:, :, None]), spread(log_dt[:, None, None]),
            b_re.reshape(flat), b_im.reshape(flat))
    outs = pl.pallas_call(
        _discretize_kernel,
        out_shape=[jax.ShapeDtypeStruct(flat, _F32)] * 4,
        name="discretize",
    )(*args)
    ar, ai, br, bi = (o.reshape(shape) for o in outs)
    return ar[:, :, 0], ai[:, :, 0], br, bi


def _mixer_kernel(x_ref, h0re_ref, h0im_ref, pool0_ref, gmix_ref, win_ref, are_ref, aim_ref,
                  bblk_ref, cre_ref, cim_ref, dskip_ref, wga_ref, wgb_ref, poolw_ref,
                  pscale_ref, wout_ref,
                  h_hbm, sre_ref, sim_ref, poolout_ref,
                  xbuf, hbuf, xsem, hsem, hs_ref, st_ref, pbuf_ref, gate_ref, u_ref, *, tc, pos0):
    i = pl.program_id(0)
    j = pl.program_id(1)
    n_j = pl.num_programs(1)
    n_steps = pl.num_programs(0) * n_j
    step = i * n_j + j
    slot = lax.rem(step, 2)
    rows = tc * STREAMS

    def x_copies(ii, jj, sl):
        return [pltpu.make_async_copy(x_ref.at[ii * STREAMS + b, pl.ds(jj * tc, tc), :],
                                      xbuf.at[sl, :, b, :], xsem.at[sl]) for b in range(STREAMS)]

    def h_copies(ii, jj, sl):
        return [pltpu.make_async_copy(hbuf.at[sl, :, b, :],
                                      h_hbm.at[ii * STREAMS + b, pl.ds(jj * tc, tc), :],
                                      hsem.at[sl]) for b in range(STREAMS)]

    @pl.when(step == 0)
    def _first_fetch():
        for cp in x_copies(i, j, slot):
            cp.start()

    @pl.when(step + 1 < n_steps)
    def _prefetch():
        wrap = j + 1 == n_j
        for cp in x_copies(jnp.where(wrap, i + 1, i), jnp.where(wrap, 0, j + 1), 1 - slot):
            cp.start()

    for cp in x_copies(i, j, slot):
        cp.wait()

    @pl.when(j == 0)
    def _load_state():
        for k in range(SSM_HALVES):
            src = slice(k * HALF_STATE, (k + 1) * HALF_STATE)
            st_ref[:, 2 * k * HALF_STATE:(2 * k + 1) * HALF_STATE] = h0re_ref[:, src]
            st_ref[:, (2 * k + 1) * HALF_STATE:(2 * k + 2) * HALF_STATE] = h0im_ref[:, src]
        pbuf_ref[0:HIST_ROWS] = pool0_ref[...]

    x = xbuf[slot].reshape(rows, D_MODEL)
    n = _rmsnorm(x, gmix_ref[...]).astype(_BF16)

    u_ref[...] = _dot(n, win_ref[:, 0:SSM_WIDTH])
    pbuf_ref[HIST_ROWS:HIST_ROWS + tc] = _dot(
        n, win_ref[:, SSM_WIDTH:SSM_WIDTH + POOL_WIDTH]).reshape(tc, STREAMS, POOL_WIDTH)
    gate0 = SSM_WIDTH + POOL_WIDTH
    for c in range(2 * D_MODEL // 512):
        gate_ref[:, c * 512:(c + 1) * 512] = jax.nn.sigmoid(
            _dot(n, win_ref[:, gate0 + c * 512:gate0 + (c + 1) * 512]))

    ub = u_ref[...].astype(_BF16)
    for k in range(SSM_HALVES):
        hs_ref[:, 2 * k * HALF_STATE:(2 * k + 2) * HALF_STATE] = _dot(
            ub[:, k * MXU_DIM:(k + 1) * MXU_DIM], bblk_ref[k])

    for k in range(SSM_HALVES):
        for c0 in range(0, HALF_STATE, SCAN_COLS):
            re_cols = slice(2 * k * HALF_STATE + c0, 2 * k * HALF_STATE + c0 + SCAN_COLS)
            im_cols = slice((2 * k + 1) * HALF_STATE + c0, (2 * k + 1) * HALF_STATE + c0 + SCAN_COLS)
            a_cols = slice(k * HALF_STATE + c0, k * HALF_STATE + c0 + SCAN_COLS)
            ar = jnp.broadcast_to(are_ref[:, a_cols], (STREAMS, SCAN_COLS))
            ai = jnp.broadcast_to(aim_ref[:, a_cols], (STREAMS, SCAN_COLS))

            def advance(t, carry, ar=ar, ai=ai, re_cols=re_cols, im_cols=im_cols):
                hr, hi = carry
                r0 = pl.multiple_of(t * STREAMS, STREAMS)
                nr = ar * hr - ai * hi + hs_ref[pl.ds(r0, STREAMS), re_cols]
                ni = ar * hi + ai * hr + hs_ref[pl.ds(r0, STREAMS), im_cols]
                hs_ref[pl.ds(r0, STREAMS), re_cols] = nr
                hs_ref[pl.ds(r0, STREAMS), im_cols] = ni
                return nr, ni

            hr, hi = lax.fori_loop(0, tc, advance, (st_ref[:, re_cols], st_ref[:, im_cols]),
                                   unroll=4)
            st_ref[:, re_cols] = hr
            st_ref[:, im_cols] = hi

    ys = []
    for k in range(SSM_HALVES):
        hre = hs_ref[:, 2 * k * HALF_STATE:(2 * k + 1) * HALF_STATE].astype(_BF16)
        him = hs_ref[:, (2 * k + 1) * HALF_STATE:(2 * k + 2) * HALF_STATE].astype(_BF16)
        ys.append(_dot(hre, cre_ref[k]) - _dot(him, cim_ref[k]))
    y = jnp.concatenate(ys, axis=1) + dskip_ref[...] * u_ref[...]
    yb = jax.nn.gelu(y).astype(_BF16)
    a_out = _dot(yb, wga_ref[...]) * jax.nn.sigmoid(_dot(yb, wgb_ref[...]))

    pos = pos0 + j * tc + lax.broadcasted_iota(jnp.int32, (tc, STREAMS, POOL_GROUP_CH), 0)
    merged = []
    for gi, w in enumerate(POOL_WINDOWS):
        c = pbuf_ref[:, :, gi * POOL_GROUP_CH:(gi + 1) * POOL_GROUP_CH]
        s, width = c, 1
        while width < w:
            s = s[:s.shape[0] - width] + s[width:]
            width *= 2
        first = HIST_ROWS + 1 - w
        wsum = s[first:first + tc]
        cnt = jnp.minimum(pos + 1, w).astype(_F32)
        pooled = wsum / cnt - c[HIST_ROWS:HIST_ROWS + tc]
        pb = pooled.reshape(rows, POOL_GROUP_CH).astype(_BF16)
        cols = slice(gi * POOL_OUT_CH, (gi + 1) * POOL_OUT_CH)
        p_out = _dot(pb, poolw_ref[gi]) * pscale_ref[:, cols]
        merged.append(gate_ref[:, cols] * a_out[:, cols]
                      + gate_ref[:, D_MODEL + gi * POOL_OUT_CH:D_MODEL + (gi + 1) * POOL_OUT_CH] * p_out)
    mb = jnp.concatenate(merged, axis=1).astype(_BF16)
    h = x + _dot(mb, wout_ref[...])
    hbuf[slot] = h.reshape(tc, STREAMS, D_MODEL)
    for cp in h_copies(i, j, slot):
        cp.start()

    @pl.when(step >= 1)
    def _drain_previous():
        first = j == 0
        for cp in h_copies(jnp.where(first, i - 1, i), jnp.where(first, n_j - 1, j - 1), 1 - slot):
            cp.wait()

    @pl.when(step == n_steps - 1)
    def _drain_last():
        for cp in h_copies(i, j, slot):
            cp.wait()

    tail = pbuf_ref[tc:tc + HIST_ROWS]
    pbuf_ref[0:HIST_ROWS] = tail

    @pl.when(j == n_j - 1)
    def _store_state():
        for k in range(SSM_HALVES):
            dst = slice(k * HALF_STATE, (k + 1) * HALF_STATE)
            sre_ref[:, dst] = st_ref[:, 2 * k * HALF_STATE:(2 * k + 1) * HALF_STATE]
            sim_ref[:, dst] = st_ref[:, (2 * k + 1) * HALF_STATE:(2 * k + 2) * HALF_STATE]
        poolout_ref[...] = tail


def _mixer(x, h0_re, h0_im, pool0, w, *, tc, pos0):
    bsz, t_len, _ = x.shape
    rows = tc * STREAMS
    grid = (bsz // STREAMS, t_len // tc)

    def full(a):
        nd = a.ndim
        return pl.BlockSpec(a.shape, lambda i, j, nd=nd: (0,) * nd)

    weights = (w["g_mix"], w["w_in"], w["abar_re"], w["abar_im"], w["bblk"], w["cre"], w["cim"],
               w["d_skip"], w["w_glu_a"], w["w_glu_b"], w["pool_w"], w["pool_scale"], w["w_out"])
    in_specs = [
        pl.BlockSpec(memory_space=pl.ANY),
        pl.BlockSpec((STREAMS, N_STATE), lambda i, j: (i, 0)),
        pl.BlockSpec((STREAMS, N_STATE), lambda i, j: (i, 0)),
        pl.BlockSpec((HIST_ROWS, STREAMS, POOL_WIDTH), lambda i, j: (0, i, 0)),
    ] + [full(a) for a in weights]
    out_specs = [
        pl.BlockSpec(memory_space=pl.ANY),
        pl.BlockSpec((STREAMS, N_STATE), lambda i, j: (i, 0)),
        pl.BlockSpec((STREAMS, N_STATE), lambda i, j: (i, 0)),
        pl.BlockSpec((HIST_ROWS, STREAMS, POOL_WIDTH), lambda i, j: (0, i, 0)),
    ]
    out_shape = [
        jax.ShapeDtypeStruct((bsz, t_len, D_MODEL), _F32),
        jax.ShapeDtypeStruct((bsz, N_STATE), _F32),
        jax.ShapeDtypeStruct((bsz, N_STATE), _F32),
        jax.ShapeDtypeStruct((HIST_ROWS, bsz, POOL_WIDTH), _F32),
    ]
    scratch = [
        pltpu.VMEM((2, tc, STREAMS, D_MODEL), _F32),
        pltpu.VMEM((2, tc, STREAMS, D_MODEL), _F32),
        pltpu.SemaphoreType.DMA((2,)),
        pltpu.SemaphoreType.DMA((2,)),
        pltpu.VMEM((rows, 2 * N_STATE), _F32),
        pltpu.VMEM((STREAMS, 2 * N_STATE), _F32),
        pltpu.VMEM((HIST_ROWS + tc, STREAMS, POOL_WIDTH), _F32),
        pltpu.VMEM((rows, 2 * D_MODEL), _F32),
        pltpu.VMEM((rows, SSM_WIDTH), _F32),
    ]
    return pl.pallas_call(
        functools.partial(_mixer_kernel, tc=tc, pos0=pos0),
        grid=grid,
        in_specs=in_specs,
        out_specs=out_specs,
        out_shape=out_shape,
        scratch_shapes=scratch,
        compiler_params=pltpu.CompilerParams(
            dimension_semantics=("arbitrary", "arbitrary"),
            vmem_limit_bytes=MIXER_VMEM_BYTES),
        name="mixer",
    )(x, h0_re, h0_im, pool0, *weights)


def _ffn_kernel(h_ref, gffn_ref, wg_ref, wu_ref, wd_ref, gfin_ref, y_ref, act_ref):
    h = h_ref[...]
    n2 = _rmsnorm(h, gffn_ref[...]).astype(_BF16)
    for c in range(D_FF // MXU_DIM):
        cols = slice(c * MXU_DIM, (c + 1) * MXU_DIM)
        act_ref[:, cols] = (jax.nn.silu(_dot(n2, wg_ref[:, cols]))
                            * _dot(n2, wu_ref[:, cols])).astype(_BF16)
    out = h + _dot(act_ref[...], wd_ref[...])
    y_ref[...] = _rmsnorm(out, gfin_ref[...])


def _ffn(h2d, w, *, tm):
    n_rows = h2d.shape[0]

    def full(a):
        nd = a.ndim
        return pl.BlockSpec(a.shape, lambda i, nd=nd: (0,) * nd)

    weights = (w["g_ffn"], w["w_ffn_gate"], w["w_ffn_up"], w["w_ffn_down"], w["g_final"])
    return pl.pallas_call(
        _ffn_kernel,
        grid=(n_rows // tm,),
        in_specs=[pl.BlockSpec((tm, D_MODEL), lambda i: (i, 0))] + [full(a) for a in weights],
        out_specs=pl.BlockSpec((tm, D_MODEL), lambda i: (i, 0)),
        out_shape=jax.ShapeDtypeStruct((n_rows, D_MODEL), _F32),
        scratch_shapes=[pltpu.VMEM((tm, D_FF), _BF16)],
        compiler_params=pltpu.CompilerParams(
            dimension_semantics=("arbitrary",),
            vmem_limit_bytes=FFN_VMEM_BYTES),
        name="ffn",
    )(h2d, *weights)


def _block_diag(m):
    g, r, c = m.shape
    eye = jnp.eye(g, dtype=m.dtype)
    return (m[:, :, None, :] * eye[:, None, :, None]).reshape(g * r, g * c)


def _prepare_weights(g_mix, w_in, ssm_a_re, ssm_a_im, ssm_log_dt, ssm_b_re, ssm_b_im, ssm_c_re,
                     ssm_c_im, ssm_d, w_glu_a, w_glu_b, pool_w, pool_scale, w_out, g_ffn,
                     w_ffn_gate, w_ffn_up, w_ffn_down, g_final):
    abar_re, abar_im, bbar_re, bbar_im = _discretize(ssm_a_re, ssm_a_im, ssm_log_dt,
                                                    ssm_b_re, ssm_b_im)
    gpb = SSM_GROUPS // SSM_HALVES
    bblk, cre, cim = [], [], []
    for k in range(SSM_HALVES):
        gs = slice(k * gpb, (k + 1) * gpb)
        b_r = _block_diag(jnp.swapaxes(bbar_re[gs], 1, 2))
        b_i = _block_diag(jnp.swapaxes(bbar_im[gs], 1, 2))
        bblk.append(jnp.concatenate([b_r, b_i], axis=1))
        cre.append(_block_diag(jnp.swapaxes(ssm_c_re[gs], 1, 2)))
        cim.append(_block_diag(jnp.swapaxes(ssm_c_im[gs], 1, 2)))
    return {
        "g_mix": g_mix.reshape(1, D_MODEL),
        "w_in": w_in.astype(_BF16),
        "abar_re": abar_re.reshape(1, N_STATE),
        "abar_im": abar_im.reshape(1, N_STATE),
        "bblk": jnp.stack(bblk).astype(_BF16),
        "cre": jnp.stack(cre).astype(_BF16),
        "cim": jnp.stack(cim).astype(_BF16),
        "d_skip": ssm_d.reshape(1, SSM_WIDTH),
        "w_glu_a": w_glu_a.astype(_BF16),
        "w_glu_b": w_glu_b.astype(_BF16),
        "pool_w": pool_w.astype(_BF16),
        "pool_scale": pool_scale.reshape(1, D_MODEL),
        "w_out": w_out.astype(_BF16),
        "g_ffn": g_ffn.reshape(1, D_MODEL),
        "w_ffn_gate": w_ffn_gate.astype(_BF16),
        "w_ffn_up": w_ffn_up.astype(_BF16),
        "w_ffn_down": w_ffn_down.astype(_BF16),
        "g_final": g_final.reshape(1, D_MODEL),
    }


def _trunk(x, h_re, h_im, pool_hist, w, *, pos0, tc, tm):
    bsz, t_len, _ = x.shape
    pool0 = jnp.pad(jnp.swapaxes(pool_hist, 0, 1), ((HIST_ROWS - POOL_HIST, 0), (0, 0), (0, 0)))
    h, s_re, s_im, pool_t = _mixer(x, h_re.reshape(bsz, N_STATE), h_im.reshape(bsz, N_STATE),
                                   pool0, w, tc=tc, pos0=pos0)
    y = _ffn(h.reshape(bsz * t_len, D_MODEL), w, tm=tm).reshape(bsz, t_len, D_MODEL)
    new_pool = jnp.swapaxes(pool_t[HIST_ROWS - POOL_HIST:], 0, 1)
    return (y,
            s_re.reshape(1, bsz, SSM_GROUPS, SSM_STATE),
            s_im.reshape(1, bsz, SSM_GROUPS, SSM_STATE),
            new_pool[None])


def kernel(x_prompt, x_sample, state_ssm_re, state_ssm_im, state_pool, g_mix, w_in, ssm_a_re, ssm_a_im, ssm_log_dt, ssm_b_re, ssm_b_im, ssm_c_re, ssm_c_im, ssm_d, w_glu_a, w_glu_b, pool_w, pool_scale, w_out, g_ffn, w_ffn_gate, w_ffn_up, w_ffn_down, g_final):
    w = _prepare_weights(g_mix[0], w_in[0], ssm_a_re[0], ssm_a_im[0], ssm_log_dt[0], ssm_b_re[0],
                         ssm_b_im[0], ssm_c_re[0], ssm_c_im[0], ssm_d[0], w_glu_a[0], w_glu_b[0],
                         pool_w[0], pool_scale[0], w_out[0], g_ffn[0], w_ffn_gate[0], w_ffn_up[0],
                         w_ffn_down[0], g_final)
    bp = x_prompt.shape[0]
    zero_ssm = jnp.zeros((bp, SSM_GROUPS, SSM_STATE), state_ssm_re.dtype)
    zero_pool = jnp.zeros((bp, POOL_HIST, POOL_WIDTH), state_pool.dtype)
    y_p, p_re, p_im, p_pool = _trunk(x_prompt, zero_ssm, zero_ssm, zero_pool, w,
                                     pos0=0, tc=64, tm=512)
    past_len = 2048
    y_s, s_re, s_im, s_pool = _trunk(x_sample, state_ssm_re[0], state_ssm_im[0], state_pool[0], w,
                                     pos0=past_len, tc=x_sample.shape[1], tm=512)
    return (y_p, y_s, p_re, p_im, p_pool, s_re, s_im, s_pool)
```

```python
import functools

import jax
import jax.numpy as jnp
from jax import lax
from jax.experimental import pallas as pl
from jax.experimental.pallas import tpu as pltpu

D_MODEL = 1024
SSM_WIDTH = 512
SSM_GROUP_CH = 16
SSM_GROUPS = 32
SSM_STATE = 64
N_STATE = SSM_GROUPS * SSM_STATE
POOL_WIDTH = 512
POOL_WINDOWS = (2, 4, 8, 16)
POOL_GROUP_CH = 128
POOL_OUT_CH = 256
POOL_HIST = 15
HIST_ROWS = 16
IN_WIDTH = SSM_WIDTH + POOL_WIDTH + 2 * D_MODEL
D_FF = 2816
RMS_EPS = 1e-6
A_RE_MAX = -1e-4

SUBLANES = 8
LANES = 128
MXU_DIM = 256
STREAMS = SUBLANES
SSM_HALVES = SSM_WIDTH // MXU_DIM
HALF_STATE = N_STATE // SSM_HALVES
SCAN_COLS = 512
MIXER_VMEM_BYTES = 52 * 1024 * 1024
FFN_VMEM_BYTES = 52 * 1024 * 1024

_F32 = jnp.float32
_BF16 = jnp.bfloat16


def _dot(a, b):
    return jnp.dot(a, b, preferred_element_type=_F32)


def _rmsnorm(x, g):
    ms = jnp.mean(x * x, axis=-1, keepdims=True)
    return x * lax.rsqrt(ms + RMS_EPS) * g


def _discretize_kernel(are_ref, aim_ref, ldt_ref, bre_ref, bim_ref,
                       oar_ref, oai_ref, obr_ref, obi_ref):
    lr = jnp.minimum(are_ref[...], A_RE_MAX)
    li = aim_ref[...]
    dt = jnp.exp(ldt_ref[...])
    mag = jnp.exp(lr * dt)
    ar = mag * jnp.cos(li * dt)
    ai = mag * jnp.sin(li * dt)
    xr = ar - 1.0
    den = lr * lr + li * li
    cr = (xr * lr + ai * li) / den
    ci = (ai * lr - xr * li) / den
    br = bre_ref[...]
    bi = bim_ref[...]
    oar_ref[...] = ar
    oai_ref[...] = ai
    obr_ref[...] = cr * br - ci * bi
    obi_ref[...] = cr * bi + ci * br


def _discretize(a_re, a_im, log_dt, b_re, b_im):
    shape = b_re.shape
    flat = (shape[0] * shape[1] * shape[2] // LANES, LANES)

    def spread(v):
        return jnp.broadcast_to(v, shape).reshape(flat)

    args = (spread(a_re[:, :, None]), spread(a_im[:, :, None]), spread(log_dt[:, None, None]),
            b_re.reshape(flat), b_im.reshape(flat))
    outs = pl.pallas_call(
        _discretize_kernel,
        out_shape=[jax.ShapeDtypeStruct(flat, _F32)] * 4,
        name="discretize",
    )(*args)
    ar, ai, br, bi = (o.reshape(shape) for o in outs)
    return ar[:, :, 0], ai[:, :, 0], br, bi


def _mixer_kernel(x_ref, h0re_ref, h0im_ref, pool0_ref, gmix_ref, win_ref, are_ref, aim_ref,
                  bblk_ref, cre_ref, cim_ref, dskip_ref, wga_ref, wgb_ref, poolw_ref,
                  pscale_ref, wout_ref,
                  h_hbm, sre_ref, sim_ref, poolout_ref,
                  xbuf, hbuf, xsem, hsem, hs_ref, st_ref, pbuf_ref, gate_ref, u_ref, *, tc, pos0):
    i = pl.program_id(0)
    j = pl.program_id(1)
    n_j = pl.num_programs(1)
    n_steps = pl.num_programs(0) * n_j
    step = i * n_j + j
    slot = lax.rem(step, 2)
    rows = tc * STREAMS

    def x_copies(ii, jj, sl):
        return [pltpu.make_async_copy(x_ref.at[ii * STREAMS + b, pl.ds(jj * tc, tc), :],
                                      xbuf.at[sl, :, b, :], xsem.at[sl]) for b in range(STREAMS)]

    def h_copies(ii, jj, sl):
        return [pltpu.make_async_copy(hbuf.at[sl, :, b, :],
                                      h_hbm.at[ii * STREAMS + b, pl.ds(jj * tc, tc), :],
                                      hsem.at[sl]) for b in range(STREAMS)]

    @pl.when(step == 0)
    def _first_fetch():
        for cp in x_copies(i, j, slot):
            cp.start()

    @pl.when(step + 1 < n_steps)
    def _prefetch():
        wrap = j + 1 == n_j
        for cp in x_copies(jnp.where(wrap, i + 1, i), jnp.where(wrap, 0, j + 1), 1 - slot):
            cp.start()

    for cp in x_copies(i, j, slot):
        cp.wait()

    @pl.when(j == 0)
    def _load_state():
        for k in range(SSM_HALVES):
            src = slice(k * HALF_STATE, (k + 1) * HALF_STATE)
            st_ref[:, 2 * k * HALF_STATE:(2 * k + 1) * HALF_STATE] = h0re_ref[:, src]
            st_ref[:, (2 * k + 1) * HALF_STATE:(2 * k + 2) * HALF_STATE] = h0im_ref[:, src]
        pbuf_ref[0:HIST_ROWS] = pool0_ref[...]

    x = xbuf[slot].reshape(rows, D_MODEL)
    n = _rmsnorm(x, gmix_ref[...]).astype(_BF16)

    u_ref[...] = _dot(n, win_ref[:, 0:SSM_WIDTH])
    pbuf_ref[HIST_ROWS:HIST_ROWS + tc] = _dot(
        n, win_ref[:, SSM_WIDTH:SSM_WIDTH + POOL_WIDTH]).reshape(tc, STREAMS, POOL_WIDTH)
    gate0 = SSM_WIDTH + POOL_WIDTH
    for c in range(2 * D_MODEL // 512):
        gate_ref[:, c * 512:(c + 1) * 512] = jax.nn.sigmoid(
            _dot(n, win_ref[:, gate0 + c * 512:gate0 + (c + 1) * 512]))

    ub = u_ref[...].astype(_BF16)
    for k in range(SSM_HALVES):
        hs_ref[:, 2 * k * HALF_STATE:(2 * k + 2) * HALF_STATE] = _dot(
            ub[:, k * MXU_DIM:(k + 1) * MXU_DIM], bblk_ref[k])

    for k in range(SSM_HALVES):
        for c0 in range(0, HALF_STATE, SCAN_COLS):
            re_cols = slice(2 * k * HALF_STATE + c0, 2 * k * HALF_STATE + c0 + SCAN_COLS)
            im_cols = slice((2 * k + 1) * HALF_STATE + c0, (2 * k + 1) * HALF_STATE + c0 + SCAN_COLS)
            a_cols = slice(k * HALF_STATE + c0, k * HALF_STATE + c0 + SCAN_COLS)
            ar = jnp.broadcast_to(are_ref[:, a_cols], (STREAMS, SCAN_COLS))
            ai = jnp.broadcast_to(aim_ref[:, a_cols], (STREAMS, SCAN_COLS))

            def advance(t, carry, ar=ar, ai=ai, re_cols=re_cols, im_cols=im_cols):
                hr, hi = carry
                r0 = pl.multiple_of(t * STREAMS, STREAMS)
                nr = ar * hr - ai * hi + hs_ref[pl.ds(r0, STREAMS), re_cols]
                ni = ar * hi + ai * hr + hs_ref[pl.ds(r0, STREAMS), im_cols]
                hs_ref[pl.ds(r0, STREAMS), re_cols] = nr
                hs_ref[pl.ds(r0, STREAMS), im_cols] = ni
                return nr, ni

            hr, hi = lax.fori_loop(0, tc, advance, (st_ref[:, re_cols], st_ref[:, im_cols]),
                                   unroll=True)
            st_ref[:, re_cols] = hr
            st_ref[:, im_cols] = hi

    ys = []
    for k in range(SSM_HALVES):
        hre = hs_ref[:, 2 * k * HALF_STATE:(2 * k + 1) * HALF_STATE].astype(_BF16)
        him = hs_ref[:, (2 * k + 1) * HALF_STATE:(2 * k + 2) * HALF_STATE].astype(_BF16)
        ys.append(_dot(hre, cre_ref[k]) - _dot(him, cim_ref[k]))
    y = jnp.concatenate(ys, axis=1) + dskip_ref[...] * u_ref[...]
    yb = jax.nn.gelu(y).astype(_BF16)
    a_out = _dot(yb, wga_ref[...]) * jax.nn.sigmoid(_dot(yb, wgb_ref[...]))

    pos = pos0 + j * tc + lax.broadcasted_iota(jnp.int32, (tc, STREAMS, POOL_GROUP_CH), 0)
    merged = []
    for gi, w in enumerate(POOL_WINDOWS):
        c = pbuf_ref[:, :, gi * POOL_GROUP_CH:(gi + 1) * POOL_GROUP_CH]
        s, width = c, 1
        while width < w:
            s = s[:s.shape[0] - width] + s[width:]
            width *= 2
        first = HIST_ROWS + 1 - w
        wsum = s[first:first + tc]
        cnt = jnp.minimum(pos + 1, w).astype(_F32)
        pooled = wsum / cnt - c[HIST_ROWS:HIST_ROWS + tc]
        pb = pooled.reshape(rows, POOL_GROUP_CH).astype(_BF16)
        cols = slice(gi * POOL_OUT_CH, (gi + 1) * POOL_OUT_CH)
        p_out = _dot(pb, poolw_ref[gi]) * pscale_ref[:, cols]
        merged.append(gate_ref[:, cols] * a_out[:, cols]
                      + gate_ref[:, D_MODEL + gi * POOL_OUT_CH:D_MODEL + (gi + 1) * POOL_OUT_CH] * p_out)
    mb = jnp.concatenate(merged, axis=1).astype(_BF16)
    h = x + _dot(mb, wout_ref[...])
    hbuf[slot] = h.reshape(tc, STREAMS, D_MODEL)
    for cp in h_copies(i, j, slot):
        cp.start()

    @pl.when(step >= 1)
    def _drain_previous():
        first = j == 0
        for cp in h_copies(jnp.where(first, i - 1, i), jnp.where(first, n_j - 1, j - 1), 1 - slot):
            cp.wait()

    @pl.when(step == n_steps - 1)
    def _drain_last():
        for cp in h_copies(i, j, slot):
            cp.wait()

    tail = pbuf_ref[tc:tc + HIST_ROWS]
    pbuf_ref[0:HIST_ROWS] = tail

    @pl.when(j == n_j - 1)
    def _store_state():
        for k in range(SSM_HALVES):
            dst = slice(k * HALF_STATE, (k + 1) * HALF_STATE)
            sre_ref[:, dst] = st_ref[:, 2 * k * HALF_STATE:(2 * k + 1) * HALF_STATE]
            sim_ref[:, dst] = st_ref[:, (2 * k + 1) * HALF_STATE:(2 * k + 2) * HALF_STATE]
        poolout_ref[...] = tail


def _mixer(x, h0_re, h0_im, pool0, w, *, tc, pos0):
    bsz, t_len, _ = x.shape
    rows = tc * STREAMS
    grid = (bsz // STREAMS, t_len // tc)

    def full(a):
        nd = a.ndim
        return pl.BlockSpec(a.shape, lambda i, j, nd=nd: (0,) * nd)

    weights = (w["g_mix"], w["w_in"], w["abar_re"], w["abar_im"], w["bblk"], w["cre"], w["cim"],
               w["d_skip"], w["w_glu_a"], w["w_glu_b"], w["pool_w"], w["pool_scale"], w["w_out"])
    in_specs = [
        pl.BlockSpec(memory_space=pl.ANY),
        pl.BlockSpec((STREAMS, N_STATE), lambda i, j: (i, 0)),
        pl.BlockSpec((STREAMS, N_STATE), lambda i, j: (i, 0)),
        pl.BlockSpec((HIST_ROWS, STREAMS, POOL_WIDTH), lambda i, j: (0, i, 0)),
    ] + [full(a) for a in weights]
    out_specs = [
        pl.BlockSpec(memory_space=pl.ANY),
        pl.BlockSpec((STREAMS, N_STATE), lambda i, j: (i, 0)),
        pl.BlockSpec((STREAMS, N_STATE), lambda i, j: (i, 0)),
        pl.BlockSpec((HIST_ROWS, STREAMS, POOL_WIDTH), lambda i, j: (0, i, 0)),
    ]
    out_shape = [
        jax.ShapeDtypeStruct((bsz, t_len, D_MODEL), _F32),
        jax.ShapeDtypeStruct((bsz, N_STATE), _F32),
        jax.ShapeDtypeStruct((bsz, N_STATE), _F32),
        jax.ShapeDtypeStruct((HIST_ROWS, bsz, POOL_WIDTH), _F32),
    ]
    scratch = [
        pltpu.VMEM((2, tc, STREAMS, D_MODEL), _F32),
        pltpu.VMEM((2, tc, STREAMS, D_MODEL), _F32),
        pltpu.SemaphoreType.DMA((2,)),
        pltpu.SemaphoreType.DMA((2,)),
        pltpu.VMEM((rows, 2 * N_STATE), _F32),
        pltpu.VMEM((STREAMS, 2 * N_STATE), _F32),
        pltpu.VMEM((HIST_ROWS + tc, STREAMS, POOL_WIDTH), _F32),
        pltpu.VMEM((rows, 2 * D_MODEL), _F32),
        pltpu.VMEM((rows, SSM_WIDTH), _F32),
    ]
    return pl.pallas_call(
        functools.partial(_mixer_kernel, tc=tc, pos0=pos0),
        grid=grid,
        in_specs=in_specs,
        out_specs=out_specs,
        out_shape=out_shape,
        scratch_shapes=scratch,
        compiler_params=pltpu.CompilerParams(
            dimension_semantics=("arbitrary", "arbitrary"),
            vmem_limit_bytes=MIXER_VMEM_BYTES),
        name="mixer",
    )(x, h0_re, h0_im, pool0, *weights)


def _ffn_kernel(h_ref, gffn_ref, wg_ref, wu_ref, wd_ref, gfin_ref, y_ref, act_ref):
    h = h_ref[...]
    n2 = _rmsnorm(h, gffn_ref[...]).astype(_BF16)
    for c in range(D_FF // MXU_DIM):
        cols = slice(c * MXU_DIM, (c + 1) * MXU_DIM)
        act_ref[:, cols] = (jax.nn.silu(_dot(n2, wg_ref[:, cols]))
                            * _dot(n2, wu_ref[:, cols])).astype(_BF16)
    out = h + _dot(act_ref[...], wd_ref[...])
    y_ref[...] = _rmsnorm(out, gfin_ref[...])


def _ffn(h2d, w, *, tm):
    n_rows = h2d.shape[0]

    def full(a):
        nd = a.ndim
        return pl.BlockSpec(a.shape, lambda i, nd=nd: (0,) * nd)

    weights = (w["g_ffn"], w["w_ffn_gate"], w["w_ffn_up"], w["w_ffn_down"], w["g_final"])
    return pl.pallas_call(
        _ffn_kernel,
        grid=(n_rows // tm,),
        in_specs=[pl.BlockSpec((tm, D_MODEL), lambda i: (i, 0))] + [full(a) for a in weights],
        out_specs=pl.BlockSpec((tm, D_MODEL), lambda i: (i, 0)),
        out_shape=jax.ShapeDtypeStruct((n_rows, D_MODEL), _F32),
        scratch_shapes=[pltpu.VMEM((tm, D_FF), _BF16)],
        compiler_params=pltpu.CompilerParams(
            dimension_semantics=("arbitrary",),
            vmem_limit_bytes=FFN_VMEM_BYTES),
        name="ffn",
    )(h2d, *weights)


def _block_diag(m):
    g, r, c = m.shape
    eye = jnp.eye(g, dtype=m.dtype)
    return (m[:, :, None, :] * eye[:, None, :, None]).reshape(g * r, g * c)


def _prepare_weights(g_mix, w_in, ssm_a_re, ssm_a_im, ssm_log_dt, ssm_b_re, ssm_b_im, ssm_c_re,
                     ssm_c_im, ssm_d, w_glu_a, w_glu_b, pool_w, pool_scale, w_out, g_ffn,
                     w_ffn_gate, w_ffn_up, w_ffn_down, g_final):
    abar_re, abar_im, bbar_re, bbar_im = _discretize(ssm_a_re, ssm_a_im, ssm_log_dt,
                                                    ssm_b_re, ssm_b_im)
    gpb = SSM_GROUPS // SSM_HALVES
    bblk, cre, cim = [], [], []
    for k in range(SSM_HALVES):
        gs = slice(k * gpb, (k + 1) * gpb)
        b_r = _block_diag(jnp.swapaxes(bbar_re[gs], 1, 2))
        b_i = _block_diag(jnp.swapaxes(bbar_im[gs], 1, 2))
        bblk.append(jnp.concatenate([b_r, b_i], axis=1))
        cre.append(_block_diag(jnp.swapaxes(ssm_c_re[gs], 1, 2)))
        cim.append(_block_diag(jnp.swapaxes(ssm_c_im[gs], 1, 2)))
    return {
        "g_mix": g_mix.reshape(1, D_MODEL),
        "w_in": w_in.astype(_BF16),
        "abar_re": abar_re.reshape(1, N_STATE),
        "abar_im": abar_im.reshape(1, N_STATE),
        "bblk": jnp.stack(bblk).astype(_BF16),
        "cre": jnp.stack(cre).astype(_BF16),
        "cim": jnp.stack(cim).astype(_BF16),
        "d_skip": ssm_d.reshape(1, SSM_WIDTH),
        "w_glu_a": w_glu_a.astype(_BF16),
        "w_glu_b": w_glu_b.astype(_BF16),
        "pool_w": pool_w.astype(_BF16),
        "pool_scale": pool_scale.reshape(1, D_MODEL),
        "w_out": w_out.astype(_BF16),
        "g_ffn": g_ffn.reshape(1, D_MODEL),
        "w_ffn_gate": w_ffn_gate.astype(_BF16),
        "w_ffn_up": w_ffn_up.astype(_BF16),
        "w_ffn_down": w_ffn_down.astype(_BF16),
        "g_final": g_final.reshape(1, D_MODEL),
    }


def _trunk(x, h_re, h_im, pool_hist, w, *, pos0, tc, tm):
    bsz, t_len, _ = x.shape
    pool0 = jnp.pad(jnp.swapaxes(pool_hist, 0, 1), ((HIST_ROWS - POOL_HIST, 0), (0, 0), (0, 0)))
    h, s_re, s_im, pool_t = _mixer(x, h_re.reshape(bsz, N_STATE), h_im.reshape(bsz, N_STATE),
                                   pool0, w, tc=tc, pos0=pos0)
    y = _ffn(h.reshape(bsz * t_len, D_MODEL), w, tm=tm).reshape(bsz, t_len, D_MODEL)
    new_pool = jnp.swapaxes(pool_t[HIST_ROWS - POOL_HIST:], 0, 1)
    return (y,
            s_re.reshape(1, bsz, SSM_GROUPS, SSM_STATE),
            s_im.reshape(1, bsz, SSM_GROUPS, SSM_STATE),
            new_pool[None])


def kernel(x_prompt, x_sample, state_ssm_re, state_ssm_im, state_pool, g_mix, w_in, ssm_a_re, ssm_a_im, ssm_log_dt, ssm_b_re, ssm_b_im, ssm_c_re, ssm_c_im, ssm_d, w_glu_a, w_glu_b, pool_w, pool_scale, w_out, g_ffn, w_ffn_gate, w_ffn_up, w_ffn_down, g_final):
    w = _prepare_weights(g_mix[0], w_in[0], ssm_a_re[0], ssm_a_im[0], ssm_log_dt[0], ssm_b_re[0],
                         ssm_b_im[0], ssm_c_re[0], ssm_c_im[0], ssm_d[0], w_glu_a[0], w_glu_b[0],
                         pool_w[0], pool_scale[0], w_out[0], g_ffn[0], w_ffn_gate[0], w_ffn_up[0],
                         w_ffn_down[0], g_final)
    bp = x_prompt.shape[0]
    zero_ssm = jnp.zeros((bp, SSM_GROUPS, SSM_STATE), state_ssm_re.dtype)
    zero_pool = jnp.zeros((bp, POOL_HIST, POOL_WIDTH), state_pool.dtype)
    y_p, p_re, p_im, p_pool = _trunk(x_prompt, zero_ssm, zero_ssm, zero_pool, w,
                                     pos0=0, tc=64, tm=512)
    past_len = 2048
    y_s, s_re, s_im, s_pool = _trunk(x_sample, state_ssm_re[0], state_ssm_im[0], state_pool[0], w,
                                     pos0=past_len, tc=x_sample.shape[1], tm=512)
    return (y_p, y_s, p_re, p_im, p_pool, s_re, s_im, s_pool)
```

```python
import functools

import jax
import jax.numpy as jnp
from jax import lax
from jax.experimental import pallas as pl
from jax.experimental.pallas import tpu as pltpu

D_MODEL = 1024
SSM_WIDTH = 512
SSM_GROUP_CH = 16
SSM_GROUPS = 32
SSM_STATE = 64
N_STATE = SSM_GROUPS * SSM_STATE
POOL_WIDTH = 512
POOL_WINDOWS = (2, 4, 8, 16)
POOL_GROUP_CH = 128
POOL_OUT_CH = 256
POOL_HIST = 15
HIST_ROWS = 16
IN_WIDTH = SSM_WIDTH + POOL_WIDTH + 2 * D_MODEL
D_FF = 2816
RMS_EPS = 1e-6
A_RE_MAX = -1e-4

SUBLANES = 8
LANES = 128
MXU_DIM = 256
STREAMS = SUBLANES
SSM_HALVES = SSM_WIDTH // MXU_DIM
HALF_STATE = N_STATE // SSM_HALVES
SCAN_COLS = 512
FFN_SUB_ROWS = 512
MIXER_VMEM_BYTES = 52 * 1024 * 1024
FFN_VMEM_BYTES = 52 * 1024 * 1024

_F32 = jnp.float32
_BF16 = jnp.bfloat16


def _dot(a, b):
    return jnp.dot(a, b, preferred_element_type=_F32)


def _rmsnorm(x, g):
    ms = jnp.mean(x * x, axis=-1, keepdims=True)
    return x * lax.rsqrt(ms + RMS_EPS) * g


def _discretize_kernel(are_ref, aim_ref, ldt_ref, bre_ref, bim_ref,
                       oar_ref, oai_ref, obr_ref, obi_ref):
    lr = jnp.minimum(are_ref[...], A_RE_MAX)
    li = aim_ref[...]
    dt = jnp.exp(ldt_ref[...])
    mag = jnp.exp(lr * dt)
    ar = mag * jnp.cos(li * dt)
    ai = mag * jnp.sin(li * dt)
    xr = ar - 1.0
    den = lr * lr + li * li
    cr = (xr * lr + ai * li) / den
    ci = (ai * lr - xr * li) / den
    br = bre_ref[...]
    bi = bim_ref[...]
    oar_ref[...] = ar
    oai_ref[...] = ai
    obr_ref[...] = cr * br - ci * bi
    obi_ref[...] = cr * bi + ci * br


def _discretize(a_re, a_im, log_dt, b_re, b_im):
    shape = b_re.shape
    flat = (shape[0] * shape[1] * shape[2] // LANES, LANES)

    def spread(v):
        return jnp.broadcast_to(v, shape).reshape(flat)

    args = (spread(a_re[:, :, None]), spread(a_im[:, :, None]), spread(log_dt[:, None, None]),
            b_re.reshape(flat), b_im.reshape(flat))
    outs = pl.pallas_call(
        _discretize_kernel,
        out_shape=[jax.ShapeDtypeStruct(flat, _F32)] * 4,
        name="discretize",
    )(*args)
    ar, ai, br, bi = (o.reshape(shape) for o in outs)
    return ar[:, :, 0], ai[:, :, 0], br, bi


def _mixer_kernel(x_ref, h0re_ref, h0im_ref, pool0_ref, gmix_ref, win_ref, are_ref, aim_ref,
                  bblk_ref, cre_ref, cim_ref, dskip_ref, wga_ref, wgb_ref, poolw_ref,
                  pscale_ref, wout_ref,
                  h_hbm, sre_ref, sim_ref, poolout_ref,
                  xbuf, hbuf, xsem, hsem, hs_ref, st_ref, pbuf_ref, n_ref, u_ref, *, tc, sub, pos0):
    i = pl.program_id(0)
    j = pl.program_id(1)
    n_j = pl.num_programs(1)
    n_steps = pl.num_programs(0) * n_j
    step = i * n_j + j
    slot = lax.rem(step, 2)
    rows = sub * STREAMS

    def x_copies(ii, jj, sl):
        return [pltpu.make_async_copy(x_ref.at[ii * STREAMS + b, pl.ds(jj * tc, tc), :],
                                      xbuf.at[sl, :, b, :], xsem.at[sl]) for b in range(STREAMS)]

    def h_copies(ii, jj, sl):
        return [pltpu.make_async_copy(hbuf.at[sl, :, b, :],
                                      h_hbm.at[ii * STREAMS + b, pl.ds(jj * tc, tc), :],
                                      hsem.at[sl]) for b in range(STREAMS)]

    @pl.when(step == 0)
    def _first_fetch():
        for cp in x_copies(i, j, slot):
            cp.start()

    @pl.when(step + 1 < n_steps)
    def _prefetch():
        wrap = j + 1 == n_j
        for cp in x_copies(jnp.where(wrap, i + 1, i), jnp.where(wrap, 0, j + 1), 1 - slot):
            cp.start()

    for cp in x_copies(i, j, slot):
        cp.wait()

    @pl.when(j == 0)
    def _load_state():
        for k in range(SSM_HALVES):
            src = slice(k * HALF_STATE, (k + 1) * HALF_STATE)
            st_ref[:, 2 * k * HALF_STATE:(2 * k + 1) * HALF_STATE] = h0re_ref[:, src]
            st_ref[:, (2 * k + 1) * HALF_STATE:(2 * k + 2) * HALF_STATE] = h0im_ref[:, src]
        pbuf_ref[0:HIST_ROWS] = pool0_ref[...]

    def subtile(q):
        f0 = q * sub
        x = xbuf[slot, f0:f0 + sub].reshape(rows, D_MODEL)
        n = _rmsnorm(x, gmix_ref[...]).astype(_BF16)
        n_ref[q] = n

        u_ref[q] = _dot(n, win_ref[:, 0:SSM_WIDTH])
        pbuf_ref[HIST_ROWS + f0:HIST_ROWS + f0 + sub] = _dot(
            n, win_ref[:, SSM_WIDTH:SSM_WIDTH + POOL_WIDTH]).reshape(sub, STREAMS, POOL_WIDTH)

        ub = u_ref[q].astype(_BF16)
        for k in range(SSM_HALVES):
            hs_ref[:, 2 * k * HALF_STATE:(2 * k + 2) * HALF_STATE] = _dot(
                ub[:, k * MXU_DIM:(k + 1) * MXU_DIM], bblk_ref[k])

        for k in range(SSM_HALVES):
            for c0 in range(0, HALF_STATE, SCAN_COLS):
                re_cols = slice(2 * k * HALF_STATE + c0, 2 * k * HALF_STATE + c0 + SCAN_COLS)
                im_cols = slice((2 * k + 1) * HALF_STATE + c0,
                                (2 * k + 1) * HALF_STATE + c0 + SCAN_COLS)
                a_cols = slice(k * HALF_STATE + c0, k * HALF_STATE + c0 + SCAN_COLS)
                ar = jnp.broadcast_to(are_ref[:, a_cols], (STREAMS, SCAN_COLS))
                ai = jnp.broadcast_to(aim_ref[:, a_cols], (STREAMS, SCAN_COLS))
                hr = st_ref[:, re_cols]
                hi = st_ref[:, im_cols]
                for t in range(sub):
                    t_rows = slice(t * STREAMS, (t + 1) * STREAMS)
                    hr, hi = (ar * hr - ai * hi + hs_ref[t_rows, re_cols],
                              ar * hi + ai * hr + hs_ref[t_rows, im_cols])
                    hs_ref[t_rows, re_cols] = hr
                    hs_ref[t_rows, im_cols] = hi
                st_ref[:, re_cols] = hr
                st_ref[:, im_cols] = hi

        ys = []
        for k in range(SSM_HALVES):
            hre = hs_ref[:, 2 * k * HALF_STATE:(2 * k + 1) * HALF_STATE].astype(_BF16)
            him = hs_ref[:, (2 * k + 1) * HALF_STATE:(2 * k + 2) * HALF_STATE].astype(_BF16)
            ys.append(_dot(hre, cre_ref[k]) - _dot(him, cim_ref[k]))
        y = jnp.concatenate(ys, axis=1) + dskip_ref[...] * u_ref[q]
        yb = jax.nn.gelu(y).astype(_BF16)

        nb = n_ref[q]
        gate0 = SSM_WIDTH + POOL_WIDTH
        pos = (pos0 + j * tc + f0
               + lax.broadcasted_iota(jnp.int32, (sub, STREAMS, POOL_GROUP_CH), 0))
        merged = []
        for gi, w in enumerate(POOL_WINDOWS):
            cols = slice(gi * POOL_OUT_CH, (gi + 1) * POOL_OUT_CH)
            a_out = _dot(yb, wga_ref[:, cols]) * jax.nn.sigmoid(_dot(yb, wgb_ref[:, cols]))
            c = pbuf_ref[f0:f0 + HIST_ROWS + sub, :, gi * POOL_GROUP_CH:(gi + 1) * POOL_GROUP_CH]
            s, width = c, 1
            while width < w:
                s = s[:s.shape[0] - width] + s[width:]
                width *= 2
            first = HIST_ROWS + 1 - w
            wsum = s[first:first + sub]
            cnt = jnp.minimum(pos + 1, w).astype(_F32)
            pooled = wsum / cnt - c[HIST_ROWS:HIST_ROWS + sub]
            pb = pooled.reshape(rows, POOL_GROUP_CH).astype(_BF16)
            p_out = _dot(pb, poolw_ref[gi]) * pscale_ref[:, cols]
            g_ssm = jax.nn.sigmoid(_dot(nb, win_ref[:, gate0 + gi * POOL_OUT_CH:
                                                    gate0 + (gi + 1) * POOL_OUT_CH]))
            g_pool = jax.nn.sigmoid(_dot(nb, win_ref[:, gate0 + D_MODEL + gi * POOL_OUT_CH:
                                                     gate0 + D_MODEL + (gi + 1) * POOL_OUT_CH]))
            merged.append(g_ssm * a_out + g_pool * p_out)
        mb = jnp.concatenate(merged, axis=1).astype(_BF16)
        h = x + _dot(mb, wout_ref[...])
        hbuf[slot, f0:f0 + sub] = h.reshape(sub, STREAMS, D_MODEL)

    for q in range(tc // sub):
        subtile(q)
    for cp in h_copies(i, j, slot):
        cp.start()

    @pl.when(step >= 1)
    def _drain_previous():
        first = j == 0
        for cp in h_copies(jnp.where(first, i - 1, i), jnp.where(first, n_j - 1, j - 1), 1 - slot):
            cp.wait()

    @pl.when(step == n_steps - 1)
    def _drain_last():
        for cp in h_copies(i, j, slot):
            cp.wait()

    tail = pbuf_ref[tc:tc + HIST_ROWS]
    pbuf_ref[0:HIST_ROWS] = tail

    @pl.when(j == n_j - 1)
    def _store_state():
        for k in range(SSM_HALVES):
            dst = slice(k * HALF_STATE, (k + 1) * HALF_STATE)
            sre_ref[:, dst] = st_ref[:, 2 * k * HALF_STATE:(2 * k + 1) * HALF_STATE]
            sim_ref[:, dst] = st_ref[:, (2 * k + 1) * HALF_STATE:(2 * k + 2) * HALF_STATE]
        poolout_ref[...] = tail


def _mixer(x, h0_re, h0_im, pool0, w, *, tc, sub, pos0):
    bsz, t_len, _ = x.shape
    rows = sub * STREAMS
    grid = (bsz // STREAMS, t_len // tc)

    def full(a):
        nd = a.ndim
        return pl.BlockSpec(a.shape, lambda i, j, nd=nd: (0,) * nd)

    weights = (w["g_mix"], w["w_in"], w["abar_re"], w["abar_im"], w["bblk"], w["cre"], w["cim"],
               w["d_skip"], w["w_glu_a"], w["w_glu_b"], w["pool_w"], w["pool_scale"], w["w_out"])
    in_specs = [
        pl.BlockSpec(memory_space=pl.ANY),
        pl.BlockSpec((STREAMS, N_STATE), lambda i, j: (i, 0)),
        pl.BlockSpec((STREAMS, N_STATE), lambda i, j: (i, 0)),
        pl.BlockSpec((HIST_ROWS, STREAMS, POOL_WIDTH), lambda i, j: (0, i, 0)),
    ] + [full(a) for a in weights]
    out_specs = [
        pl.BlockSpec(memory_space=pl.ANY),
        pl.BlockSpec((STREAMS, N_STATE), lambda i, j: (i, 0)),
        pl.BlockSpec((STREAMS, N_STATE), lambda i, j: (i, 0)),
        pl.BlockSpec((HIST_ROWS, STREAMS, POOL_WIDTH), lambda i, j: (0, i, 0)),
    ]
    out_shape = [
        jax.ShapeDtypeStruct((bsz, t_len, D_MODEL), _F32),
        jax.ShapeDtypeStruct((bsz, N_STATE), _F32),
        jax.ShapeDtypeStruct((bsz, N_STATE), _F32),
        jax.ShapeDtypeStruct((HIST_ROWS, bsz, POOL_WIDTH), _F32),
    ]
    scratch = [
        pltpu.VMEM((2, tc, STREAMS, D_MODEL), _F32),
        pltpu.VMEM((2, tc, STREAMS, D_MODEL), _F32),
        pltpu.SemaphoreType.DMA((2,)),
        pltpu.SemaphoreType.DMA((2,)),
        pltpu.VMEM((rows, 2 * N_STATE), _F32),
        pltpu.VMEM((STREAMS, 2 * N_STATE), _F32),
        pltpu.VMEM((HIST_ROWS + tc, STREAMS, POOL_WIDTH), _F32),
        pltpu.VMEM((tc // sub, rows, D_MODEL), _BF16),
        pltpu.VMEM((tc // sub, rows, SSM_WIDTH), _F32),
    ]
    return pl.pallas_call(
        functools.partial(_mixer_kernel, tc=tc, sub=sub, pos0=pos0),
        grid=grid,
        in_specs=in_specs,
        out_specs=out_specs,
        out_shape=out_shape,
        scratch_shapes=scratch,
        compiler_params=pltpu.CompilerParams(
            dimension_semantics=("arbitrary", "arbitrary"),
            vmem_limit_bytes=MIXER_VMEM_BYTES),
        name="mixer",
    )(x, h0_re, h0_im, pool0, *weights)


def _ffn_kernel(h_ref, gffn_ref, wg_ref, wu_ref, wd_ref, gfin_ref, y_ref, act_ref):
    for r0 in range(0, h_ref.shape[0], FFN_SUB_ROWS):
        rs = slice(r0, r0 + FFN_SUB_ROWS)
        h = h_ref[rs, :]
        n2 = _rmsnorm(h, gffn_ref[...]).astype(_BF16)
        for c in range(D_FF // MXU_DIM):
            cols = slice(c * MXU_DIM, (c + 1) * MXU_DIM)
            act_ref[rs, cols] = (jax.nn.silu(_dot(n2, wg_ref[:, cols]))
                                 * _dot(n2, wu_ref[:, cols])).astype(_BF16)
        out = h + _dot(act_ref[rs, :], wd_ref[...])
        y_ref[rs, :] = _rmsnorm(out, gfin_ref[...])


def _ffn(h2d, w, *, tm):
    n_rows = h2d.shape[0]

    def full(a):
        nd = a.ndim
        return pl.BlockSpec(a.shape, lambda i, nd=nd: (0,) * nd)

    weights = (w["g_ffn"], w["w_ffn_gate"], w["w_ffn_up"], w["w_ffn_down"], w["g_final"])
    return pl.pallas_call(
        _ffn_kernel,
        grid=(n_rows // tm,),
        in_specs=[pl.BlockSpec((tm, D_MODEL), lambda i: (i, 0))] + [full(a) for a in weights],
        out_specs=pl.BlockSpec((tm, D_MODEL), lambda i: (i, 0)),
        out_shape=jax.ShapeDtypeStruct((n_rows, D_MODEL), _F32),
        scratch_shapes=[pltpu.VMEM((tm, D_FF), _BF16)],
        compiler_params=pltpu.CompilerParams(
            dimension_semantics=("arbitrary",),
            vmem_limit_bytes=FFN_VMEM_BYTES),
        name="ffn",
    )(h2d, *weights)


def _block_diag(m):
    g, r, c = m.shape
    eye = jnp.eye(g, dtype=m.dtype)
    return (m[:, :, None, :] * eye[:, None, :, None]).reshape(g * r, g * c)


def _prepare_weights(g_mix, w_in, ssm_a_re, ssm_a_im, ssm_log_dt, ssm_b_re, ssm_b_im, ssm_c_re,
                     ssm_c_im, ssm_d, w_glu_a, w_glu_b, pool_w, pool_scale, w_out, g_ffn,
                     w_ffn_gate, w_ffn_up, w_ffn_down, g_final):
    abar_re, abar_im, bbar_re, bbar_im = _discretize(ssm_a_re, ssm_a_im, ssm_log_dt,
                                                    ssm_b_re, ssm_b_im)
    gpb = SSM_GROUPS // SSM_HALVES
    bblk, cre, cim = [], [], []
    for k in range(SSM_HALVES):
        gs = slice(k * gpb, (k + 1) * gpb)
        b_r = _block_diag(jnp.swapaxes(bbar_re[gs], 1, 2))
        b_i = _block_diag(jnp.swapaxes(bbar_im[gs], 1, 2))
        bblk.append(jnp.concatenate([b_r, b_i], axis=1))
        cre.append(_block_diag(jnp.swapaxes(ssm_c_re[gs], 1, 2)))
        cim.append(_block_diag(jnp.swapaxes(ssm_c_im[gs], 1, 2)))
    return {
        "g_mix": g_mix.reshape(1, D_MODEL),
        "w_in": w_in.astype(_BF16),
        "abar_re": abar_re.reshape(1, N_STATE),
        "abar_im": abar_im.reshape(1, N_STATE),
        "bblk": jnp.stack(bblk).astype(_BF16),
        "cre": jnp.stack(cre).astype(_BF16),
        "cim": jnp.stack(cim).astype(_BF16),
        "d_skip": ssm_d.reshape(1, SSM_WIDTH),
        "w_glu_a": w_glu_a.astype(_BF16),
        "w_glu_b": w_glu_b.astype(_BF16),
        "pool_w": pool_w.astype(_BF16),
        "pool_scale": pool_scale.reshape(1, D_MODEL),
        "w_out": w_out.astype(_BF16),
        "g_ffn": g_ffn.reshape(1, D_MODEL),
        "w_ffn_gate": w_ffn_gate.astype(_BF16),
        "w_ffn_up": w_ffn_up.astype(_BF16),
        "w_ffn_down": w_ffn_down.astype(_BF16),
        "g_final": g_final.reshape(1, D_MODEL),
    }


def _trunk(x, h_re, h_im, pool_hist, w, *, pos0, tc, sub, tm):
    bsz, t_len, _ = x.shape
    pool0 = jnp.pad(jnp.swapaxes(pool_hist, 0, 1), ((HIST_ROWS - POOL_HIST, 0), (0, 0), (0, 0)))
    h, s_re, s_im, pool_t = _mixer(x, h_re.reshape(bsz, N_STATE), h_im.reshape(bsz, N_STATE),
                                   pool0, w, tc=tc, sub=sub, pos0=pos0)
    y = _ffn(h.reshape(bsz * t_len, D_MODEL), w, tm=tm).reshape(bsz, t_len, D_MODEL)
    new_pool = jnp.swapaxes(pool_t[HIST_ROWS - POOL_HIST:], 0, 1)
    return (y,
            s_re.reshape(1, bsz, SSM_GROUPS, SSM_STATE),
            s_im.reshape(1, bsz, SSM_GROUPS, SSM_STATE),
            new_pool[None])


def kernel(x_prompt, x_sample, state_ssm_re, state_ssm_im, state_pool, g_mix, w_in, ssm_a_re, ssm_a_im, ssm_log_dt, ssm_b_re, ssm_b_im, ssm_c_re, ssm_c_im, ssm_d, w_glu_a, w_glu_b, pool_w, pool_scale, w_out, g_ffn, w_ffn_gate, w_ffn_up, w_ffn_down, g_final):
    w = _prepare_weights(g_mix[0], w_in[0], ssm_a_re[0], ssm_a_im[0], ssm_log_dt[0], ssm_b_re[0],
                         ssm_b_im[0], ssm_c_re[0], ssm_c_im[0], ssm_d[0], w_glu_a[0], w_glu_b[0],
                         pool_w[0], pool_scale[0], w_out[0], g_ffn[0], w_ffn_gate[0], w_ffn_up[0],
                         w_ffn_down[0], g_final)
    bp = x_prompt.shape[0]
    zero_ssm = jnp.zeros((bp, SSM_GROUPS, SSM_STATE), state_ssm_re.dtype)
    zero_pool = jnp.zeros((bp, POOL_HIST, POOL_WIDTH), state_pool.dtype)
    y_p, p_re, p_im, p_pool = _trunk(x_prompt, zero_ssm, zero_ssm, zero_pool, w,
                                     pos0=0, tc=128, sub=64, tm=1024)
    past_len = 2048
    y_s, s_re, s_im, s_pool = _trunk(x_sample, state_ssm_re[0], state_ssm_im[0], state_pool[0], w,
                                     pos0=past_len, tc=x_sample.shape[1], sub=x_sample.shape[1],
                                     tm=512)
    return (y_p, y_s, p_re, p_im, p_pool, s_re, s_im, s_pool)
```

```python
import functools

import jax
import jax.numpy as jnp
from jax import lax
from jax.experimental import pallas as pl
from jax.experimental.pallas import tpu as pltpu

D_MODEL = 1024
SSM_WIDTH = 512
SSM_GROUP_CH = 16
SSM_GROUPS = 32
SSM_STATE = 64
N_STATE = SSM_GROUPS * SSM_STATE
POOL_WIDTH = 512
POOL_WINDOWS = (2, 4, 8, 16)
POOL_GROUP_CH = 128
POOL_OUT_CH = 256
POOL_HIST = 15
HIST_ROWS = 16
IN_WIDTH = SSM_WIDTH + POOL_WIDTH + 2 * D_MODEL
D_FF = 2816
RMS_EPS = 1e-6
A_RE_MAX = -1e-4

SUBLANES = 8
LANES = 128
MXU_DIM = 256
STREAMS = SUBLANES
SSM_HALVES = SSM_WIDTH // MXU_DIM
HALF_STATE = N_STATE // SSM_HALVES
FFN_SUB_ROWS = 512
MIXER_VMEM_BYTES = 52 * 1024 * 1024
FFN_VMEM_BYTES = 52 * 1024 * 1024

_F32 = jnp.float32
_BF16 = jnp.bfloat16


def _dot(a, b):
    return jnp.dot(a, b, preferred_element_type=_F32)


def _rmsnorm(x, g):
    ms = jnp.mean(x * x, axis=-1, keepdims=True)
    return x * lax.rsqrt(ms + RMS_EPS) * g


def _discretize_kernel(are_ref, aim_ref, ldt_ref, bre_ref, bim_ref, cim_ref,
                       oar_ref, oai_ref, obr_ref, obi_ref, ocn_ref):
    lr = jnp.minimum(are_ref[...], A_RE_MAX)
    li = aim_ref[...]
    dt = jnp.exp(ldt_ref[...])
    mag = jnp.exp(lr * dt)
    ar = mag * jnp.cos(li * dt)
    ai = mag * jnp.sin(li * dt)
    xr = ar - 1.0
    den = lr * lr + li * li
    cr = (xr * lr + ai * li) / den
    ci = (ai * lr - xr * li) / den
    br = bre_ref[...]
    bi = bim_ref[...]
    oar_ref[...] = ar
    oai_ref[...] = ai
    obr_ref[...] = cr * br - ci * bi
    obi_ref[...] = cr * bi + ci * br
    ocn_ref[...] = -cim_ref[...]


def _discretize(a_re, a_im, log_dt, b_re, b_im, c_im):
    shape = b_re.shape
    flat = (shape[0] * shape[1] * shape[2] // LANES, LANES)

    def spread(v):
        return jnp.broadcast_to(v, shape).reshape(flat)

    args = (spread(a_re[:, :, None]), spread(a_im[:, :, None]), spread(log_dt[:, None, None]),
            b_re.reshape(flat), b_im.reshape(flat), c_im.reshape(flat))
    outs = pl.pallas_call(
        _discretize_kernel,
        out_shape=[jax.ShapeDtypeStruct(flat, _F32)] * 5,
        name="discretize",
    )(*args)
    ar, ai, br, bi = (o.reshape(shape) for o in outs[:4])
    return ar[:, :, 0], ai[:, :, 0], br, bi, outs[4].reshape(c_im.shape)


def _mixer_kernel(x_ref, h0_ref, pool0_ref, gmix_ref, win_ref, are_ref, aim_ref,
                  bblk_ref, cblk_ref, dskip_ref, wga_ref, wgb_ref, poolw_ref,
                  pscale_ref, wout_ref,
                  h_hbm, sout_ref, poolout_ref,
                  xbuf, hbuf, xsem, hsem, hs_ref, st_ref, pbuf_ref, n_ref, u_ref, *, tc, sub, pos0):
    i = pl.program_id(0)
    j = pl.program_id(1)
    n_j = pl.num_programs(1)
    n_steps = pl.num_programs(0) * n_j
    step = i * n_j + j
    slot = lax.rem(step, 2)
    rows = sub * STREAMS

    def x_copies(ii, jj, sl):
        return [pltpu.make_async_copy(x_ref.at[ii * STREAMS + b, pl.ds(jj * tc, tc), :],
                                      xbuf.at[sl, :, b, :], xsem.at[sl]) for b in range(STREAMS)]

    def h_copies(ii, jj, sl):
        return [pltpu.make_async_copy(hbuf.at[sl, :, b, :],
                                      h_hbm.at[ii * STREAMS + b, pl.ds(jj * tc, tc), :],
                                      hsem.at[sl]) for b in range(STREAMS)]

    @pl.when(step == 0)
    def _first_fetch():
        for cp in x_copies(i, j, slot):
            cp.start()

    @pl.when(step + 1 < n_steps)
    def _prefetch():
        wrap = j + 1 == n_j
        for cp in x_copies(jnp.where(wrap, i + 1, i), jnp.where(wrap, 0, j + 1), 1 - slot):
            cp.start()

    for cp in x_copies(i, j, slot):
        cp.wait()

    @pl.when(j == 0)
    def _load_state():
        st_ref[...] = h0_ref[...]
        pbuf_ref[0:HIST_ROWS] = pool0_ref[...]

    def subtile(q):
        f0 = q * sub
        x = xbuf[slot, f0:f0 + sub].reshape(rows, D_MODEL)
        n = _rmsnorm(x, gmix_ref[...]).astype(_BF16)
        n_ref[q] = n

        u_ref[q] = _dot(n, win_ref[:, 0:SSM_WIDTH])
        pbuf_ref[HIST_ROWS + f0:HIST_ROWS + f0 + sub] = _dot(
            n, win_ref[:, SSM_WIDTH:SSM_WIDTH + POOL_WIDTH]).reshape(sub, STREAMS, POOL_WIDTH)

        ub = u_ref[q].astype(_BF16)
        for k in range(SSM_HALVES):
            hs_ref[:, 2 * k * HALF_STATE:(2 * k + 2) * HALF_STATE] = _dot(
                ub[:, k * MXU_DIM:(k + 1) * MXU_DIM], bblk_ref[k])

        for p in range(N_STATE // LANES):
            re_cols = slice(2 * p * LANES, (2 * p + 1) * LANES)
            im_cols = slice((2 * p + 1) * LANES, (2 * p + 2) * LANES)
            a_cols = slice(p * LANES, (p + 1) * LANES)
            ar = jnp.broadcast_to(are_ref[:, a_cols], (STREAMS, LANES))
            ai = jnp.broadcast_to(aim_ref[:, a_cols], (STREAMS, LANES))
            hr = st_ref[p, 0]
            hi = st_ref[p, 1]
            for t in range(sub):
                t_rows = slice(t * STREAMS, (t + 1) * STREAMS)
                hr, hi = (ar * hr - ai * hi + hs_ref[t_rows, re_cols],
                          ar * hi + ai * hr + hs_ref[t_rows, im_cols])
                hs_ref[t_rows, re_cols] = hr
                hs_ref[t_rows, im_cols] = hi
            st_ref[p, 0] = hr
            st_ref[p, 1] = hi

        ys = []
        for k in range(SSM_HALVES):
            hk = hs_ref[:, 2 * k * HALF_STATE:(2 * k + 2) * HALF_STATE].astype(_BF16)
            ys.append(_dot(hk, cblk_ref[k]))
        y = jnp.concatenate(ys, axis=1) + dskip_ref[...] * u_ref[q]
        yb = jax.nn.gelu(y).astype(_BF16)

        nb = n_ref[q]
        gate0 = SSM_WIDTH + POOL_WIDTH
        pos = (pos0 + j * tc + f0
               + lax.broadcasted_iota(jnp.int32, (sub, STREAMS, POOL_GROUP_CH), 0))
        merged = []
        for gi, w in enumerate(POOL_WINDOWS):
            cols = slice(gi * POOL_OUT_CH, (gi + 1) * POOL_OUT_CH)
            a_out = _dot(yb, wga_ref[:, cols]) * jax.nn.sigmoid(_dot(yb, wgb_ref[:, cols]))
            c = pbuf_ref[f0:f0 + HIST_ROWS + sub, :, gi * POOL_GROUP_CH:(gi + 1) * POOL_GROUP_CH]
            s, width = c, 1
            while width < w:
                s = s[:s.shape[0] - width] + s[width:]
                width *= 2
            first = HIST_ROWS + 1 - w
            wsum = s[first:first + sub]
            cnt = jnp.minimum(pos + 1, w).astype(_F32)
            pooled = wsum / cnt - c[HIST_ROWS:HIST_ROWS + sub]
            pb = pooled.reshape(rows, POOL_GROUP_CH).astype(_BF16)
            p_out = _dot(pb, poolw_ref[gi]) * pscale_ref[:, cols]
            g_ssm = jax.nn.sigmoid(_dot(nb, win_ref[:, gate0 + gi * POOL_OUT_CH:
                                                    gate0 + (gi + 1) * POOL_OUT_CH]))
            g_pool = jax.nn.sigmoid(_dot(nb, win_ref[:, gate0 + D_MODEL + gi * POOL_OUT_CH:
                                                     gate0 + D_MODEL + (gi + 1) * POOL_OUT_CH]))
            merged.append(g_ssm * a_out + g_pool * p_out)
        mb = jnp.concatenate(merged, axis=1).astype(_BF16)
        h = x + _dot(mb, wout_ref[...])
        hbuf[slot, f0:f0 + sub] = h.reshape(sub, STREAMS, D_MODEL)

    for q in range(tc // sub):
        subtile(q)
    for cp in h_copies(i, j, slot):
        cp.start()

    @pl.when(step >= 1)
    def _drain_previous():
        first = j == 0
        for cp in h_copies(jnp.where(first, i - 1, i), jnp.where(first, n_j - 1, j - 1), 1 - slot):
            cp.wait()

    @pl.when(step == n_steps - 1)
    def _drain_last():
        for cp in h_copies(i, j, slot):
            cp.wait()

    tail = pbuf_ref[tc:tc + HIST_ROWS]
    pbuf_ref[0:HIST_ROWS] = tail

    @pl.when(j == n_j - 1)
    def _store_state():
        sout_ref[...] = st_ref[...]
        poolout_ref[...] = tail


def _mixer(x, h0, pool0, w, *, tc, sub, pos0):
    bsz, t_len, _ = x.shape
    rows = sub * STREAMS
    grid = (bsz // STREAMS, t_len // tc)
    state_block = (N_STATE // LANES, 2, STREAMS, LANES)

    def full(a):
        nd = a.ndim
        return pl.BlockSpec(a.shape, lambda i, j, nd=nd: (0,) * nd)

    weights = (w["g_mix"], w["w_in"], w["abar_re"], w["abar_im"], w["bblk"], w["cblk"],
               w["d_skip"], w["w_glu_a"], w["w_glu_b"], w["pool_w"], w["pool_scale"], w["w_out"])
    in_specs = [
        pl.BlockSpec(memory_space=pl.ANY),
        pl.BlockSpec(state_block, lambda i, j: (0, 0, i, 0)),
        pl.BlockSpec((HIST_ROWS, STREAMS, POOL_WIDTH), lambda i, j: (0, i, 0)),
    ] + [full(a) for a in weights]
    out_specs = [
        pl.BlockSpec(memory_space=pl.ANY),
        pl.BlockSpec(state_block, lambda i, j: (0, 0, i, 0)),
        pl.BlockSpec((HIST_ROWS, STREAMS, POOL_WIDTH), lambda i, j: (0, i, 0)),
    ]
    out_shape = [
        jax.ShapeDtypeStruct((bsz, t_len, D_MODEL), _F32),
        jax.ShapeDtypeStruct(h0.shape, _F32),
        jax.ShapeDtypeStruct((HIST_ROWS, bsz, POOL_WIDTH), _F32),
    ]
    scratch = [
        pltpu.VMEM((2, tc, STREAMS, D_MODEL), _F32),
        pltpu.VMEM((2, tc, STREAMS, D_MODEL), _F32),
        pltpu.SemaphoreType.DMA((2,)),
        pltpu.SemaphoreType.DMA((2,)),
        pltpu.VMEM((rows, 2 * N_STATE), _F32),
        pltpu.VMEM(state_block, _F32),
        pltpu.VMEM((HIST_ROWS + tc, STREAMS, POOL_WIDTH), _F32),
        pltpu.VMEM((tc // sub, rows, D_MODEL), _BF16),
        pltpu.VMEM((tc // sub, rows, SSM_WIDTH), _F32),
    ]
    return pl.pallas_call(
        functools.partial(_mixer_kernel, tc=tc, sub=sub, pos0=pos0),
        grid=grid,
        in_specs=in_specs,
        out_specs=out_specs,
        out_shape=out_shape,
        scratch_shapes=scratch,
        compiler_params=pltpu.CompilerParams(
            dimension_semantics=("arbitrary", "arbitrary"),
            vmem_limit_bytes=MIXER_VMEM_BYTES),
        name="mixer",
    )(x, h0, pool0, *weights)


def _ffn_kernel(h_ref, gffn_ref, wg_ref, wu_ref, wd_ref, gfin_ref, y_ref, act_ref):
    for r0 in range(0, h_ref.shape[0], FFN_SUB_ROWS):
        rs = slice(r0, r0 + FFN_SUB_ROWS)
        h = h_ref[rs, :]
        n2 = _rmsnorm(h, gffn_ref[...]).astype(_BF16)
        for c in range(D_FF // MXU_DIM):
            cols = slice(c * MXU_DIM, (c + 1) * MXU_DIM)
            act_ref[rs, cols] = (jax.nn.silu(_dot(n2, wg_ref[:, cols]))
                                 * _dot(n2, wu_ref[:, cols])).astype(_BF16)
        out = h + _dot(act_ref[rs, :], wd_ref[...])
        y_ref[rs, :] = _rmsnorm(out, gfin_ref[...])


def _ffn(h2d, w, *, tm):
    n_rows = h2d.shape[0]

    def full(a):
        nd = a.ndim
        return pl.BlockSpec(a.shape, lambda i, nd=nd: (0,) * nd)

    weights = (w["g_ffn"], w["w_ffn_gate"], w["w_ffn_up"], w["w_ffn_down"], w["g_final"])
    return pl.pallas_call(
        _ffn_kernel,
        grid=(n_rows // tm,),
        in_specs=[pl.BlockSpec((tm, D_MODEL), lambda i: (i, 0))] + [full(a) for a in weights],
        out_specs=pl.BlockSpec((tm, D_MODEL), lambda i: (i, 0)),
        out_shape=jax.ShapeDtypeStruct((n_rows, D_MODEL), _F32),
        scratch_shapes=[pltpu.VMEM((tm, D_FF), _BF16)],
        compiler_params=pltpu.CompilerParams(
            dimension_semantics=("arbitrary",),
            vmem_limit_bytes=FFN_VMEM_BYTES),
        name="ffn",
    )(h2d, *weights)


def _block_diag(m):
    g, r, c = m.shape
    eye = jnp.eye(g, dtype=m.dtype)
    return (m[:, :, None, :] * eye[:, None, :, None]).reshape(g * r, g * c)


def _interleave(re, im, axis):
    shape = re.shape
    split = shape[:axis] + (shape[axis] // LANES, LANES) + shape[axis + 1:]
    both = jnp.stack([re.reshape(split), im.reshape(split)], axis=axis + 1)
    return both.reshape(shape[:axis] + (2 * shape[axis],) + shape[axis + 1:])


def _prepare_weights(g_mix, w_in, ssm_a_re, ssm_a_im, ssm_log_dt, ssm_b_re, ssm_b_im, ssm_c_re,
                     ssm_c_im, ssm_d, w_glu_a, w_glu_b, pool_w, pool_scale, w_out, g_ffn,
                     w_ffn_gate, w_ffn_up, w_ffn_down, g_final):
    abar_re, abar_im, bbar_re, bbar_im, c_im_neg = _discretize(
        ssm_a_re, ssm_a_im, ssm_log_dt, ssm_b_re, ssm_b_im, ssm_c_im)
    gpb = SSM_GROUPS // SSM_HALVES
    bblk, cblk = [], []
    for k in range(SSM_HALVES):
        gs = slice(k * gpb, (k + 1) * gpb)
        b_r = _block_diag(jnp.swapaxes(bbar_re[gs], 1, 2))
        b_i = _block_diag(jnp.swapaxes(bbar_im[gs], 1, 2))
        bblk.append(_interleave(b_r, b_i, axis=1))
        c_r = _block_diag(jnp.swapaxes(ssm_c_re[gs], 1, 2))
        c_i = _block_diag(jnp.swapaxes(c_im_neg[gs], 1, 2))
        cblk.append(_interleave(c_r, c_i, axis=0))
    return {
        "g_mix": g_mix.reshape(1, D_MODEL),
        "w_in": w_in.astype(_BF16),
        "abar_re": abar_re.reshape(1, N_STATE),
        "abar_im": abar_im.reshape(1, N_STATE),
        "bblk": jnp.stack(bblk).astype(_BF16),
        "cblk": jnp.stack(cblk).astype(_BF16),
        "d_skip": ssm_d.reshape(1, SSM_WIDTH),
        "w_glu_a": w_glu_a.astype(_BF16),
        "w_glu_b": w_glu_b.astype(_BF16),
        "pool_w": pool_w.astype(_BF16),
        "pool_scale": pool_scale.reshape(1, D_MODEL),
        "w_out": w_out.astype(_BF16),
        "g_ffn": g_ffn.reshape(1, D_MODEL),
        "w_ffn_gate": w_ffn_gate.astype(_BF16),
        "w_ffn_up": w_ffn_up.astype(_BF16),
        "w_ffn_down": w_ffn_down.astype(_BF16),
        "g_final": g_final.reshape(1, D_MODEL),
    }


def _trunk(x, h_re, h_im, pool_hist, w, *, pos0, tc, sub, tm):
    bsz, t_len, _ = x.shape
    pool0 = jnp.pad(jnp.swapaxes(pool_hist, 0, 1), ((HIST_ROWS - POOL_HIST, 0), (0, 0), (0, 0)))
    tiles = (bsz, N_STATE // LANES, LANES)
    h0 = jnp.transpose(jnp.stack([h_re.reshape(tiles), h_im.reshape(tiles)]), (2, 0, 1, 3))
    h, s_out, pool_t = _mixer(x, h0, pool0, w, tc=tc, sub=sub, pos0=pos0)
    s_re, s_im = jnp.transpose(s_out, (1, 2, 0, 3))
    y = _ffn(h.reshape(bsz * t_len, D_MODEL), w, tm=tm).reshape(bsz, t_len, D_MODEL)
    new_pool = jnp.swapaxes(pool_t[HIST_ROWS - POOL_HIST:], 0, 1)
    return (y,
            s_re.reshape(1, bsz, SSM_GROUPS, SSM_STATE),
            s_im.reshape(1, bsz, SSM_GROUPS, SSM_STATE),
            new_pool[None])


def kernel(x_prompt, x_sample, state_ssm_re, state_ssm_im, state_pool, g_mix, w_in, ssm_a_re, ssm_a_im, ssm_log_dt, ssm_b_re, ssm_b_im, ssm_c_re, ssm_c_im, ssm_d, w_glu_a, w_glu_b, pool_w, pool_scale, w_out, g_ffn, w_ffn_gate, w_ffn_up, w_ffn_down, g_final):
    w = _prepare_weights(g_mix[0], w_in[0], ssm_a_re[0], ssm_a_im[0], ssm_log_dt[0], ssm_b_re[0],
                         ssm_b_im[0], ssm_c_re[0], ssm_c_im[0], ssm_d[0], w_glu_a[0], w_glu_b[0],
                         pool_w[0], pool_scale[0], w_out[0], g_ffn[0], w_ffn_gate[0], w_ffn_up[0],
                         w_ffn_down[0], g_final)
    bp = x_prompt.shape[0]
    zero_ssm = jnp.zeros((bp, SSM_GROUPS, SSM_STATE), state_ssm_re.dtype)
    zero_pool = jnp.zeros((bp, POOL_HIST, POOL_WIDTH), state_pool.dtype)
    y_p, p_re, p_im, p_pool = _trunk(x_prompt, zero_ssm, zero_ssm, zero_pool, w,
                                     pos0=0, tc=128, sub=64, tm=1024)
    past_len = 2048
    y_s, s_re, s_im, s_pool = _trunk(x_sample, state_ssm_re[0], state_ssm_im[0], state_pool[0], w,
                                     pos0=past_len, tc=x_sample.shape[1], sub=x_sample.shape[1],
                                     tm=512)
    return (y_p, y_s, p_re, p_im, p_pool, s_re, s_im, s_pool)
```

```python
import functools

import jax
import jax.numpy as jnp
from jax import lax
from jax.experimental import pallas as pl
from jax.experimental.pallas import tpu as pltpu

D_MODEL = 1024
SSM_WIDTH = 512
SSM_GROUP_CH = 16
SSM_GROUPS = 32
SSM_STATE = 64
N_STATE = SSM_GROUPS * SSM_STATE
POOL_WIDTH = 512
POOL_WINDOWS = (2, 4, 8, 16)
POOL_GROUP_CH = 128
POOL_OUT_CH = 256
POOL_HIST = 15
HIST_ROWS = 16
IN_WIDTH = SSM_WIDTH + POOL_WIDTH + 2 * D_MODEL
D_FF = 2816
RMS_EPS = 1e-6
A_RE_MAX = -1e-4

SUBLANES = 8
LANES = 128
MXU_DIM = 256
STREAMS = SUBLANES
SSM_HALVES = SSM_WIDTH // MXU_DIM
HALF_STATE = N_STATE // SSM_HALVES
FFN_SUB_ROWS = 512
MIXER_VMEM_BYTES = 52 * 1024 * 1024
FFN_VMEM_BYTES = 52 * 1024 * 1024

_F32 = jnp.float32
_BF16 = jnp.bfloat16


def _dot(a, b):
    return jnp.dot(a, b, preferred_element_type=_F32)


def _rmsnorm(x, g):
    ms = jnp.mean(x * x, axis=-1, keepdims=True)
    return x * lax.rsqrt(ms + RMS_EPS) * g


def _discretize_kernel(are_ref, aim_ref, ldt_ref, bre_ref, bim_ref, cim_ref,
                       oar_ref, oai_ref, obr_ref, obi_ref, ocn_ref):
    lr = jnp.minimum(are_ref[...], A_RE_MAX)
    li = aim_ref[...]
    dt = jnp.exp(ldt_ref[...])
    mag = jnp.exp(lr * dt)
    ar = mag * jnp.cos(li * dt)
    ai = mag * jnp.sin(li * dt)
    xr = ar - 1.0
    den = lr * lr + li * li
    cr = (xr * lr + ai * li) / den
    ci = (ai * lr - xr * li) / den
    br = bre_ref[...]
    bi = bim_ref[...]
    oar_ref[...] = ar
    oai_ref[...] = ai
    obr_ref[...] = cr * br - ci * bi
    obi_ref[...] = cr * bi + ci * br
    ocn_ref[...] = -cim_ref[...]


def _discretize(a_re, a_im, log_dt, b_re, b_im, c_im):
    g, p = a_re.shape
    a_sds = jax.ShapeDtypeStruct((g, 1, p), _F32)
    b_sds = jax.ShapeDtypeStruct(c_im.shape, _F32)
    return pl.pallas_call(
        _discretize_kernel,
        out_shape=[a_sds, a_sds, b_sds, b_sds, b_sds],
        name="discretize",
    )(a_re.reshape(g, 1, p), a_im.reshape(g, 1, p), log_dt.reshape(g, 1, 1),
      jnp.swapaxes(b_re, 1, 2), jnp.swapaxes(b_im, 1, 2), c_im)


def _mixer_kernel(x_ref, h0_ref, pool0_ref, gmix_ref, win_ref, are_ref, aim_ref,
                  bblk_ref, cblk_ref, dskip_ref, wga_ref, wgb_ref, poolw_ref,
                  pscale_ref, wout_ref,
                  h_hbm, sout_ref, poolout_ref,
                  xbuf, hbuf, xsem, hsem, hs_ref, st_ref, pbuf_ref, n_ref, u_ref, *, tc, sub, pos0):
    i = pl.program_id(0)
    j = pl.program_id(1)
    n_j = pl.num_programs(1)
    n_steps = pl.num_programs(0) * n_j
    step = i * n_j + j
    slot = lax.rem(step, 2)
    rows = sub * STREAMS

    def x_copies(ii, jj, sl):
        return [pltpu.make_async_copy(x_ref.at[ii * STREAMS + b, pl.ds(jj * tc, tc), :],
                                      xbuf.at[sl, :, b, :], xsem.at[sl]) for b in range(STREAMS)]

    def h_copies(ii, jj, sl):
        return [pltpu.make_async_copy(hbuf.at[sl, :, b, :],
                                      h_hbm.at[ii * STREAMS + b, pl.ds(jj * tc, tc), :],
                                      hsem.at[sl]) for b in range(STREAMS)]

    @pl.when(step == 0)
    def _first_fetch():
        for cp in x_copies(i, j, slot):
            cp.start()

    @pl.when(step + 1 < n_steps)
    def _prefetch():
        wrap = j + 1 == n_j
        for cp in x_copies(jnp.where(wrap, i + 1, i), jnp.where(wrap, 0, j + 1), 1 - slot):
            cp.start()

    for cp in x_copies(i, j, slot):
        cp.wait()

    @pl.when(j == 0)
    def _load_state():
        st_ref[...] = h0_ref[...]
        pbuf_ref[0:HIST_ROWS] = pool0_ref[...]

    def subtile(q):
        f0 = q * sub
        x = xbuf[slot, f0:f0 + sub].reshape(rows, D_MODEL)
        n = _rmsnorm(x, gmix_ref[...]).astype(_BF16)
        n_ref[q] = n

        u_ref[q] = _dot(n, win_ref[:, 0:SSM_WIDTH])
        pbuf_ref[HIST_ROWS + f0:HIST_ROWS + f0 + sub] = _dot(
            n, win_ref[:, SSM_WIDTH:SSM_WIDTH + POOL_WIDTH]).reshape(sub, STREAMS, POOL_WIDTH)

        ub = u_ref[q].astype(_BF16)
        for k in range(SSM_HALVES):
            hs_ref[:, 2 * k * HALF_STATE:(2 * k + 2) * HALF_STATE] = _dot(
                ub[:, k * MXU_DIM:(k + 1) * MXU_DIM], bblk_ref[k])

        for p in range(N_STATE // LANES):
            re_cols = slice(2 * p * LANES, (2 * p + 1) * LANES)
            im_cols = slice((2 * p + 1) * LANES, (2 * p + 2) * LANES)
            a_cols = slice(p * LANES, (p + 1) * LANES)
            ar = jnp.broadcast_to(are_ref[:, a_cols], (STREAMS, LANES))
            ai = jnp.broadcast_to(aim_ref[:, a_cols], (STREAMS, LANES))
            hr = st_ref[p, 0]
            hi = st_ref[p, 1]
            for t in range(sub):
                t_rows = slice(t * STREAMS, (t + 1) * STREAMS)
                hr, hi = (ar * hr - ai * hi + hs_ref[t_rows, re_cols],
                          ar * hi + ai * hr + hs_ref[t_rows, im_cols])
                hs_ref[t_rows, re_cols] = hr
                hs_ref[t_rows, im_cols] = hi
            st_ref[p, 0] = hr
            st_ref[p, 1] = hi

        ys = []
        for k in range(SSM_HALVES):
            hk = hs_ref[:, 2 * k * HALF_STATE:(2 * k + 2) * HALF_STATE].astype(_BF16)
            ys.append(_dot(hk, cblk_ref[k]))
        y = jnp.concatenate(ys, axis=1) + dskip_ref[...] * u_ref[q]
        yb = jax.nn.gelu(y).astype(_BF16)

        nb = n_ref[q]
        gate0 = SSM_WIDTH + POOL_WIDTH
        pos = (pos0 + j * tc + f0
               + lax.broadcasted_iota(jnp.int32, (sub, STREAMS, POOL_GROUP_CH), 0))
        merged = []
        for gi, w in enumerate(POOL_WINDOWS):
            cols = slice(gi * POOL_OUT_CH, (gi + 1) * POOL_OUT_CH)
            a_out = _dot(yb, wga_ref[:, cols]) * jax.nn.sigmoid(_dot(yb, wgb_ref[:, cols]))
            c = pbuf_ref[f0:f0 + HIST_ROWS + sub, :, gi * POOL_GROUP_CH:(gi + 1) * POOL_GROUP_CH]
            s, width = c, 1
            while width < w:
                s = s[:s.shape[0] - width] + s[width:]
                width *= 2
            first = HIST_ROWS + 1 - w
            wsum = s[first:first + sub]
            cnt = jnp.minimum(pos + 1, w).astype(_F32)
            pooled = wsum / cnt - c[HIST_ROWS:HIST_ROWS + sub]
            pb = pooled.reshape(rows, POOL_GROUP_CH).astype(_BF16)
            p_out = _dot(pb, poolw_ref[gi]) * pscale_ref[:, cols]
            g_ssm = jax.nn.sigmoid(_dot(nb, win_ref[:, gate0 + gi * POOL_OUT_CH:
                                                    gate0 + (gi + 1) * POOL_OUT_CH]))
            g_pool = jax.nn.sigmoid(_dot(nb, win_ref[:, gate0 + D_MODEL + gi * POOL_OUT_CH:
                                                     gate0 + D_MODEL + (gi + 1) * POOL_OUT_CH]))
            merged.append(g_ssm * a_out + g_pool * p_out)
        mb = jnp.concatenate(merged, axis=1).astype(_BF16)
        h = x + _dot(mb, wout_ref[...])
        hbuf[slot, f0:f0 + sub] = h.reshape(sub, STREAMS, D_MODEL)

    for q in range(tc // sub):
        subtile(q)
    for cp in h_copies(i, j, slot):
        cp.start()

    @pl.when(step >= 1)
    def _drain_previous():
        first = j == 0
        for cp in h_copies(jnp.where(first, i - 1, i), jnp.where(first, n_j - 1, j - 1), 1 - slot):
            cp.wait()

    @pl.when(step == n_steps - 1)
    def _drain_last():
        for cp in h_copies(i, j, slot):
            cp.wait()

    tail = pbuf_ref[tc:tc + HIST_ROWS]
    pbuf_ref[0:HIST_ROWS] = tail

    @pl.when(j == n_j - 1)
    def _store_state():
        sout_ref[...] = st_ref[...]
        poolout_ref[...] = tail


def _mixer(x, h0, pool0, w, *, tc, sub, pos0):
    bsz, t_len, _ = x.shape
    rows = sub * STREAMS
    grid = (bsz // STREAMS, t_len // tc)
    state_block = (N_STATE // LANES, 2, STREAMS, LANES)

    def full(a):
        nd = a.ndim
        return pl.BlockSpec(a.shape, lambda i, j, nd=nd: (0,) * nd)

    weights = (w["g_mix"], w["w_in"], w["abar_re"], w["abar_im"], w["bblk"], w["cblk"],
               w["d_skip"], w["w_glu_a"], w["w_glu_b"], w["pool_w"], w["pool_scale"], w["w_out"])
    in_specs = [
        pl.BlockSpec(memory_space=pl.ANY),
        pl.BlockSpec(state_block, lambda i, j: (0, 0, i, 0)),
        pl.BlockSpec((HIST_ROWS, STREAMS, POOL_WIDTH), lambda i, j: (0, i, 0)),
    ] + [full(a) for a in weights]
    out_specs = [
        pl.BlockSpec(memory_space=pl.ANY),
        pl.BlockSpec(state_block, lambda i, j: (0, 0, i, 0)),
        pl.BlockSpec((HIST_ROWS, STREAMS, POOL_WIDTH), lambda i, j: (0, i, 0)),
    ]
    out_shape = [
        jax.ShapeDtypeStruct((bsz, t_len, D_MODEL), _F32),
        jax.ShapeDtypeStruct(h0.shape, _F32),
        jax.ShapeDtypeStruct((HIST_ROWS, bsz, POOL_WIDTH), _F32),
    ]
    scratch = [
        pltpu.VMEM((2, tc, STREAMS, D_MODEL), _F32),
        pltpu.VMEM((2, tc, STREAMS, D_MODEL), _F32),
        pltpu.SemaphoreType.DMA((2,)),
        pltpu.SemaphoreType.DMA((2,)),
        pltpu.VMEM((rows, 2 * N_STATE), _F32),
        pltpu.VMEM(state_block, _F32),
        pltpu.VMEM((HIST_ROWS + tc, STREAMS, POOL_WIDTH), _F32),
        pltpu.VMEM((tc // sub, rows, D_MODEL), _BF16),
        pltpu.VMEM((tc // sub, rows, SSM_WIDTH), _F32),
    ]
    return pl.pallas_call(
        functools.partial(_mixer_kernel, tc=tc, sub=sub, pos0=pos0),
        grid=grid,
        in_specs=in_specs,
        out_specs=out_specs,
        out_shape=out_shape,
        scratch_shapes=scratch,
        compiler_params=pltpu.CompilerParams(
            dimension_semantics=("arbitrary", "arbitrary"),
            vmem_limit_bytes=MIXER_VMEM_BYTES),
        name="mixer",
    )(x, h0, pool0, *weights)


def _ffn_kernel(h_ref, gffn_ref, wg_ref, wu_ref, wd_ref, gfin_ref, y_ref, act_ref):
    for r0 in range(0, h_ref.shape[0], FFN_SUB_ROWS):
        rs = slice(r0, r0 + FFN_SUB_ROWS)
        h = h_ref[rs, :]
        n2 = _rmsnorm(h, gffn_ref[...]).astype(_BF16)
        for c in range(D_FF // MXU_DIM):
            cols = slice(c * MXU_DIM, (c + 1) * MXU_DIM)
            act_ref[rs, cols] = (jax.nn.silu(_dot(n2, wg_ref[:, cols]))
                                 * _dot(n2, wu_ref[:, cols])).astype(_BF16)
        out = h + _dot(act_ref[rs, :], wd_ref[...])
        y_ref[rs, :] = _rmsnorm(out, gfin_ref[...])


def _ffn(h2d, w, *, tm):
    n_rows = h2d.shape[0]

    def full(a):
        nd = a.ndim
        return pl.BlockSpec(a.shape, lambda i, nd=nd: (0,) * nd)

    weights = (w["g_ffn"], w["w_ffn_gate"], w["w_ffn_up"], w["w_ffn_down"], w["g_final"])
    return pl.pallas_call(
        _ffn_kernel,
        grid=(n_rows // tm,),
        in_specs=[pl.BlockSpec((tm, D_MODEL), lambda i: (i, 0))] + [full(a) for a in weights],
        out_specs=pl.BlockSpec((tm, D_MODEL), lambda i: (i, 0)),
        out_shape=jax.ShapeDtypeStruct((n_rows, D_MODEL), _F32),
        scratch_shapes=[pltpu.VMEM((tm, D_FF), _BF16)],
        compiler_params=pltpu.CompilerParams(
            dimension_semantics=("arbitrary",),
            vmem_limit_bytes=FFN_VMEM_BYTES),
        name="ffn",
    )(h2d, *weights)


def _block_diag(m):
    k, g, r, c = m.shape
    eye = jnp.eye(g, dtype=m.dtype)
    return (m[:, :, :, None, :] * eye[None, :, None, :, None]).reshape(k, g * r, g * c)


def _interleave(re, im, axis):
    shape = re.shape
    split = shape[:axis] + (shape[axis] // LANES, LANES) + shape[axis + 1:]
    both = jnp.stack([re.reshape(split), im.reshape(split)], axis=axis + 1)
    return both.reshape(shape[:axis] + (2 * shape[axis],) + shape[axis + 1:])


def _prepare_weights(g_mix, w_in, ssm_a_re, ssm_a_im, ssm_log_dt, ssm_b_re, ssm_b_im, ssm_c_re,
                     ssm_c_im, ssm_d, w_glu_a, w_glu_b, pool_w, pool_scale, w_out, g_ffn,
                     w_ffn_gate, w_ffn_up, w_ffn_down, g_final):
    abar_re, abar_im, bbar_re, bbar_im, c_im_neg = _discretize(
        ssm_a_re, ssm_a_im, ssm_log_dt, ssm_b_re, ssm_b_im, ssm_c_im)
    gpb = SSM_GROUPS // SSM_HALVES

    def halves(m):
        return m.reshape((SSM_HALVES, gpb) + m.shape[1:])

    bblk = _interleave(_block_diag(halves(bbar_re)), _block_diag(halves(bbar_im)), axis=2)
    cblk = _interleave(_block_diag(halves(jnp.swapaxes(ssm_c_re, 1, 2))),
                       _block_diag(halves(jnp.swapaxes(c_im_neg, 1, 2))), axis=1)
    return {
        "g_mix": g_mix.reshape(1, D_MODEL),
        "w_in": w_in.astype(_BF16),
        "abar_re": abar_re.reshape(1, N_STATE),
        "abar_im": abar_im.reshape(1, N_STATE),
        "bblk": bblk.astype(_BF16),
        "cblk": cblk.astype(_BF16),
        "d_skip": ssm_d.reshape(1, SSM_WIDTH),
        "w_glu_a": w_glu_a.astype(_BF16),
        "w_glu_b": w_glu_b.astype(_BF16),
        "pool_w": pool_w.astype(_BF16),
        "pool_scale": pool_scale.reshape(1, D_MODEL),
        "w_out": w_out.astype(_BF16),
        "g_ffn": g_ffn.reshape(1, D_MODEL),
        "w_ffn_gate": w_ffn_gate.astype(_BF16),
        "w_ffn_up": w_ffn_up.astype(_BF16),
        "w_ffn_down": w_ffn_down.astype(_BF16),
        "g_final": g_final.reshape(1, D_MODEL),
    }


def _trunk(x, h_re, h_im, pool_hist, w, *, pos0, tc, sub, tm):
    bsz, t_len, _ = x.shape
    pool0 = jnp.pad(jnp.swapaxes(pool_hist, 0, 1), ((HIST_ROWS - POOL_HIST, 0), (0, 0), (0, 0)))
    tiles = (bsz, N_STATE // LANES, LANES)
    h0 = jnp.transpose(jnp.stack([h_re.reshape(tiles), h_im.reshape(tiles)]), (2, 0, 1, 3))
    h, s_out, pool_t = _mixer(x, h0, pool0, w, tc=tc, sub=sub, pos0=pos0)
    s_re, s_im = jnp.transpose(s_out, (1, 2, 0, 3))
    y = _ffn(h.reshape(bsz * t_len, D_MODEL), w, tm=tm).reshape(bsz, t_len, D_MODEL)
    new_pool = jnp.swapaxes(pool_t[HIST_ROWS - POOL_HIST:], 0, 1)
    return (y,
            s_re.reshape(1, bsz, SSM_GROUPS, SSM_STATE),
            s_im.reshape(1, bsz, SSM_GROUPS, SSM_STATE),
            new_pool[None])


def kernel(x_prompt, x_sample, state_ssm_re, state_ssm_im, state_pool, g_mix, w_in, ssm_a_re, ssm_a_im, ssm_log_dt, ssm_b_re, ssm_b_im, ssm_c_re, ssm_c_im, ssm_d, w_glu_a, w_glu_b, pool_w, pool_scale, w_out, g_ffn, w_ffn_gate, w_ffn_up, w_ffn_down, g_final):
    w = _prepare_weights(g_mix[0], w_in[0], ssm_a_re[0], ssm_a_im[0], ssm_log_dt[0], ssm_b_re[0],
                         ssm_b_im[0], ssm_c_re[0], ssm_c_im[0], ssm_d[0], w_glu_a[0], w_glu_b[0],
                         pool_w[0], pool_scale[0], w_out[0], g_ffn[0], w_ffn_gate[0], w_ffn_up[0],
                         w_ffn_down[0], g_final)
    bp = x_prompt.shape[0]
    zero_ssm = jnp.zeros((bp, SSM_GROUPS, SSM_STATE), state_ssm_re.dtype)
    zero_pool = jnp.zeros((bp, POOL_HIST, POOL_WIDTH), state_pool.dtype)
    y_p, p_re, p_im, p_pool = _trunk(x_prompt, zero_ssm, zero_ssm, zero_pool, w,
                                     pos0=0, tc=128, sub=64, tm=1024)
    past_len = 2048
    y_s, s_re, s_im, s_pool = _trunk(x_sample, state_ssm_re[0], state_ssm_im[0], state_pool[0], w,
                                     pos0=past_len, tc=x_sample.shape[1], sub=x_sample.shape[1],
                                     tm=1024)
    return (y_p, y_s, p_re, p_im, p_pool, s_re, s_im, s_pool)
```
